```python
import math
import jax, jax.numpy as jnp
from jax import lax
import numpy as np

D_MODEL = 1024
BATCH = 4
SEQ = 4096
DEPTH = 1
DEC_BATCH = 32
DEC_SEQ = 4
PAST_LEN = 8192
PAGE_SIZE = 128

ATT_HEADS = 8
ATT_DH = 64
ATT_VD = 2 * ATT_DH
QK_WIDTH = ATT_HEADS * 2 * ATT_DH
ATT_WIDTH = ATT_HEADS * ATT_VD
ROPE_THETA = 10000.0
Q_BLOCK = 128
GM_GROUPS = 8
GM_CHUNK = 128
GM_WIDTH = 1024
GM_GC = GM_WIDTH // GM_GROUPS
SPLITS = (QK_WIDTH, 2 * QK_WIDTH, 2 * QK_WIDTH + ATT_WIDTH, 2 * QK_WIDTH + ATT_WIDTH + 2 * GM_WIDTH)
IN_WIDTH = SPLITS[-1] + 2 * D_MODEL
PEER_HEADS = 8
PEER_NKEYS = 128
PEER_EXPERTS = PEER_NKEYS * PEER_NKEYS
PEER_DKEY = 256
PEER_DHALF = PEER_DKEY // 2
PEER_TOPK = 16
PEER_TOK_BLOCK = 256
EPS = 1e-6

kernel_name = "hybrid_diffattn_gmlp_peer_step"


def lambda_init_fn(layer):
    return 0.8 - 0.6 * math.exp(-0.3 * layer)


def rms_norm(x, g):
    xf = x.astype(jnp.float32)
    y = xf * lax.rsqrt(jnp.mean(xf * xf, axis=-1, keepdims=True) + EPS)
    return (y * g.astype(jnp.float32)).astype(x.dtype)


def rope(x, pos):
    half = ATT_DH // 2
    inv = 1.0 / (ROPE_THETA ** (jnp.arange(0, ATT_DH, 2, dtype=jnp.float32) / ATT_DH))
    ang = pos.astype(jnp.float32)[:, None] * inv[None, :]
    cos = jnp.cos(ang)[None, :, None, None, :]
    sin = jnp.sin(ang)[None, :, None, None, :]
    xf = x.astype(jnp.float32)
    x1, x2 = xf[..., :half], xf[..., half:]
    return jnp.concatenate([x1 * cos - x2 * sin, x2 * cos + x1 * sin], axis=-1).astype(x.dtype)


def adaln(c, w_ada, b_ada):
    m = jax.nn.silu(c) @ w_ada + b_ada
    return jnp.split(m[:, None, :], 6, axis=-1)


def modulate(x, g, shift, scale):
    return rms_norm(x, g) * (1 + scale) + shift


def diff_lambda(lq1, lk1, lq2, lk2, lambda_init):
    f = lambda a: a.astype(jnp.float32)
    return jnp.exp(jnp.sum(f(lq1) * f(lk1))) - jnp.exp(jnp.sum(f(lq2) * f(lk2))) + lambda_init


def diff_attend(q, k, v, q_pos, k_pos, lam):
    s = jnp.einsum('bqhmd,bkhmd->bhmqk', q, k, preferred_element_type=jnp.float32) * (ATT_DH ** -0.5)
    s = jnp.where(k_pos[None, :] <= q_pos[:, None], s, -jnp.inf)
    p = jax.nn.softmax(s, axis=-1)
    a = p[:, :, 0] - lam * p[:, :, 1]
    return jnp.einsum('bhqk,bkhe->bqhe', a.astype(v.dtype), v)


def spatial_gating(zuv, v_g, ws, bs):
    B, T, _ = zuv.shape
    u, v = jnp.split(jax.nn.gelu(zuv, approximate=False), 2, axis=-1)
    v = rms_norm(v.reshape(B, T, GM_GROUPS, GM_GC), v_g)
    nc = -(-T // GM_CHUNK)
    vp = jnp.pad(v, ((0, 0), (0, nc * GM_CHUNK - T), (0, 0), (0, 0))).reshape(B, nc, GM_CHUNK, GM_GROUPS, GM_GC)
    w = ws * jnp.tril(jnp.ones((GM_CHUNK, GM_CHUNK), ws.dtype))
    mixed = jnp.einsum('gij,bcjgd->bcigd', w, vp) + bs.T[:, :, None]
    mixed = mixed.reshape(B, nc * GM_CHUNK, GM_GROUPS, GM_GC)[:, :T].reshape(B, T, GM_WIDTH)
    return u * mixed, v.reshape(B, T, GM_WIDTH)


def peer(h, w_q, subkeys, u_tab, v_tab):
    lead = h.shape[:-1]
    xt = h.reshape(-1, D_MODEL)
    n = xt.shape[0]
    blk = min(PEER_TOK_BLOCK, n)
    nb = -(-n // blk)
    xt = jnp.pad(xt, ((0, nb * blk - n), (0, 0)))

    def one(xb):
        q = (xb @ w_q).reshape(blk, PEER_HEADS, 2, PEER_DHALF)
        s = jnp.einsum('thpd,hpkd->thpk', q, subkeys, preferred_element_type=jnp.float32)
        s1, i1 = lax.top_k(s[:, :, 0], PEER_TOPK)
        s2, i2 = lax.top_k(s[:, :, 1], PEER_TOPK)
        cand = (s1[..., :, None] + s2[..., None, :]).reshape(blk, PEER_HEADS, PEER_TOPK * PEER_TOPK)
        cidx = (i1[..., :, None] * PEER_NKEYS + i2[..., None, :]).reshape(blk, PEER_HEADS, PEER_TOPK * PEER_TOPK)
        top_s, sel = lax.top_k(cand, PEER_TOPK)
        idx = jnp.take_along_axis(cidx, sel, axis=-1)
        g = jax.nn.softmax(top_s, axis=-1)
        a = jnp.einsum('td,thkd->thk', xb, u_tab[idx], preferred_element_type=jnp.float32)
        wgt = (g * jax.nn.gelu(a, approximate=False)).astype(xb.dtype)
        return jnp.einsum('thk,thkd->td', wgt, v_tab[idx])

    out = lax.map(one, xt.reshape(nb, blk, D_MODEL))
    return out.reshape(nb * blk, D_MODEL)[:n].reshape(*lead, D_MODEL)


def decoder_layer(x, c, pos, attend, lambda_init, p):
    B, T, _ = x.shape
    sh1, sc1, g1, sh2, sc2, g2 = adaln(c, p['w_ada'], p['b_ada'])
    h = modulate(x, p['norm1_g'], sh1, sc1)
    z = h @ p['w_in']
    zq, zk, zv, zuv, zg = jnp.split(z, SPLITS, axis=-1)
    q = rope(rms_norm(zq.reshape(B, T, ATT_HEADS, 2, ATT_DH), p['q_norm_g']), pos)
    k = rope(rms_norm(zk.reshape(B, T, ATT_HEADS, 2, ATT_DH), p['k_norm_g']), pos)
    v = zv.reshape(B, T, ATT_HEADS, ATT_VD)
    lam = diff_lambda(p['lambda_q1'], p['lambda_k1'], p['lambda_q2'], p['lambda_k2'], lambda_init)
    o = attend(q, k, v, lam)
    a_out = (rms_norm(o, p['subln_g']) * (1 - lambda_init)).reshape(B, T, ATT_WIDTH)
    s_out, v_rows = spatial_gating(zuv, p['gm_v_norm_g'], p['gm_ws'], p['gm_bs'])
    ga, gb = jnp.split(jax.nn.sigmoid(zg), 2, axis=-1)
    mix = (ga * (a_out @ p['w_att_out']) + gb * (s_out @ p['w_gm_out'])) @ p['w_out']
    x = x + g1 * mix
    h2 = modulate(x, p['norm2_g'], sh2, sc2)
    x = x + g2 * peer(h2, p['peer_w_q'], p['peer_subkeys'], p['peer_u'], p['peer_v'])
    return x, k, v, v_rows


def setup_inputs(seed: int = 0) -> dict:
    key = jax.random.key(seed)
    ks = jax.random.split(key, 32)
    f32 = jnp.float32

    def nrm(i, shape, std):
        return std * jax.random.normal(ks[i], shape, f32)

    def gain(i, shape):
        return 1.0 + nrm(i, shape, 0.01)

    n_pages = PAST_LEN // PAGE_SIZE
    used = DEC_BATCH * n_pages
    n_pool = used + used // 4
    page_table = jax.random.permutation(ks[4], n_pool)[:used].reshape(DEC_BATCH, n_pages).astype(jnp.int32)
    L = DEPTH
    D = D_MODEL
    return {
        'x_prompt': nrm(0, (BATCH, SEQ, D), 1.0),
        'x_sample': nrm(1, (DEC_BATCH, DEC_SEQ, D), 1.0),
        'cache_k': nrm(2, (L, n_pool, PAGE_SIZE, ATT_HEADS, 2, ATT_DH), 1.0),
        'cache_v': nrm(3, (L, n_pool, PAGE_SIZE, ATT_HEADS, ATT_VD), 1.0),
        'page_table': page_table,
        'c_prompt': nrm(5, (BATCH, D), 1.0),
        'c_sample': nrm(6, (DEC_BATCH, D), 1.0),
        'w_ada': nrm(7, (L, D, 6 * D), 0.5 * D ** -0.5),
        'b_ada': nrm(8, (L, 6 * D), 0.01),
        'norm1_g': gain(9, (L, D)),
        'w_in': nrm(10, (L, D, IN_WIDTH), D ** -0.5),
        'q_norm_g': gain(11, (L, ATT_DH)),
        'k_norm_g': gain(12, (L, ATT_DH)),
        'lambda_q1': nrm(13, (L, ATT_DH), 0.1),
        'lambda_k1': nrm(14, (L, ATT_DH), 0.1),
        'lambda_q2': nrm(15, (L, ATT_DH), 0.1),
        'lambda_k2': nrm(16, (L, ATT_DH), 0.1),
        'subln_g': gain(17, (L, ATT_VD)),
        'gm_v_norm_g': gain(18, (L, GM_GC)),
        'gm_ws': nrm(19, (L, GM_GROUPS, GM_CHUNK, GM_CHUNK), GM_CHUNK ** -0.5),
        'gm_bs': gain(20, (L, GM_GROUPS, GM_CHUNK)),
        'w_att_out': nrm(21, (L, ATT_WIDTH, D), ATT_WIDTH ** -0.5),
        'w_gm_out': nrm(22, (L, GM_WIDTH, D), GM_WIDTH ** -0.5),
        'w_out': nrm(23, (L, D, D), D ** -0.5),
        'norm2_g': gain(24, (L, D)),
        'peer_w_q': nrm(25, (L, D, PEER_HEADS * PEER_DKEY), D ** -0.5),
        'peer_subkeys': nrm(26, (L, PEER_HEADS, 2, PEER_NKEYS, PEER_DHALF), PEER_DHALF ** -0.5),
        'peer_u': nrm(27, (L, PEER_EXPERTS, D), D ** -0.5),
        'peer_v': nrm(28, (L, PEER_EXPERTS, D), PEER_HEADS ** -0.5),
    }


def reference(x_prompt, x_sample, cache_k, cache_v, page_table, c_prompt, c_sample,
              w_ada, b_ada, norm1_g, w_in, q_norm_g, k_norm_g,
              lambda_q1, lambda_k1, lambda_q2, lambda_k2, subln_g,
              gm_v_norm_g, gm_ws, gm_bs, w_att_out, w_gm_out, w_out, norm2_g,
              peer_w_q, peer_subkeys, peer_u, peer_v):
    T_p = x_prompt.shape[1]
    T_s = x_sample.shape[1]
    past_len = page_table.shape[1] * cache_k.shape[2]
    pos_p = jnp.arange(T_p)
    pos_s = past_len + jnp.arange(T_s)
    k_pos_s = jnp.arange(past_len + T_s)
    yp, ys = x_prompt, x_sample
    kp_list, vp_list, ks_list, vs_list, gm_list = [], [], [], [], []
    for l in range(DEPTH):
        p = {
            'w_ada': w_ada[l], 'b_ada': b_ada[l], 'norm1_g': norm1_g[l], 'w_in': w_in[l],
            'q_norm_g': q_norm_g[l], 'k_norm_g': k_norm_g[l],
            'lambda_q1': lambda_q1[l], 'lambda_k1': lambda_k1[l],
            'lambda_q2': lambda_q2[l], 'lambda_k2': lambda_k2[l], 'subln_g': subln_g[l],
            'gm_v_norm_g': gm_v_norm_g[l], 'gm_ws': gm_ws[l], 'gm_bs': gm_bs[l],
            'w_att_out': w_att_out[l], 'w_gm_out': w_gm_out[l], 'w_out': w_out[l],
            'norm2_g': norm2_g[l], 'peer_w_q': peer_w_q[l], 'peer_subkeys': peer_subkeys[l],
            'peer_u': peer_u[l], 'peer_v': peer_v[l],
        }
        lam_init = lambda_init_fn(l)

        def attend_prompt(q, k, v, lam):
            B = q.shape[0]
            nq = T_p // Q_BLOCK
            qb = jnp.moveaxis(q.reshape(B, nq, Q_BLOCK, ATT_HEADS, 2, ATT_DH), 1, 0)
            pb = pos_p.reshape(nq, Q_BLOCK)
            o = lax.map(lambda qp: diff_attend(qp[0], k, v, qp[1], pos_p, lam), (qb, pb))
            return jnp.moveaxis(o, 0, 1).reshape(B, T_p, ATT_HEADS, ATT_VD)

        def attend_sample(q, k, v, lam):
            DB = q.shape[0]
            pk = cache_k[l][page_table].reshape(DB, past_len, ATT_HEADS, 2, ATT_DH)
            pv = cache_v[l][page_table].reshape(DB, past_len, ATT_HEADS, ATT_VD)
            k_all = jnp.concatenate([pk, k], axis=1)
            v_all = jnp.concatenate([pv, v], axis=1)
            return diff_attend(q, k_all, v_all, pos_s, k_pos_s, lam)

        yp, kp, vp, _ = decoder_layer(yp, c_prompt, pos_p, attend_prompt, lam_init, p)
        ys, ksm, vsm, gmv = decoder_layer(ys, c_sample, pos_s, attend_sample, lam_init, p)
        kp_list.append(kp)
        vp_list.append(vp)
        ks_list.append(ksm)
        vs_list.append(vsm)
        gm_list.append(gmv)
    new_k_prompt = jnp.stack(kp_list, axis=0)
    new_v_prompt = jnp.stack(vp_list, axis=0)
    new_k_sample = jnp.stack(ks_list, axis=0)
    new_v_sample = jnp.stack(vs_list, axis=0)
    new_gm_v_sample = jnp.stack(gm_list, axis=0)
    return (yp, ys, new_k_prompt, new_v_prompt, new_k_sample, new_v_sample, new_gm_v_sample)
```

```python
import functools
import math

import jax
import jax.numpy as jnp
from jax import lax
from jax.experimental import pallas as pl
from jax.experimental.pallas import tpu as pltpu

BF = jnp.bfloat16
F32 = jnp.float32
I32 = jnp.int32

D_MODEL = 1024
ATT_HEADS = 8
ATT_DH = 64
ATT_VD = 128
ROPE_THETA = 10000.0
GM_GROUPS = 8
GM_CHUNK = 128
GM_WIDTH = 1024
IN_WIDTH = 7168
PEER_HEADS = 8
PEER_NKEYS = 128
PEER_TOPK = 16
PEER_SEL = PEER_HEADS * PEER_TOPK
EPS = 1e-6
NEG_INF = float("-inf")

VMEM_LIMIT = 56 * 1024 * 1024


def _cparams(sem):
    return pltpu.CompilerParams(dimension_semantics=sem, vmem_limit_bytes=VMEM_LIMIT)


def _gelu(x):
    return 0.5 * x * (1.0 + lax.erf(x * math.sqrt(0.5)))


def _seg_mean_sq(z, seg, n):
    z2 = z * z
    hi = z2.astype(BF)
    lo = (z2 - hi.astype(F32)).astype(BF)
    ss = jnp.dot(hi, seg, preferred_element_type=F32) + jnp.dot(lo, seg, preferred_element_type=F32)
    return ss * (1.0 / n)


def _ada_kernel(c_ref, w_ref, b_ref, o_ref):
    c = c_ref[...]
    s = c * jax.nn.sigmoid(c)
    o_ref[...] = jnp.dot(s.astype(BF), w_ref[...].astype(BF), preferred_element_type=F32) + b_ref[...]


def _adaln(c_all, w_ada, b_ada):
    r = c_all.shape[0]
    n = w_ada.shape[1]
    tn = 1024
    return pl.pallas_call(
        _ada_kernel,
        grid=(n // tn,),
        in_specs=[
            pl.BlockSpec((r, D_MODEL), lambda j: (0, 0)),
            pl.BlockSpec((D_MODEL, tn), lambda j: (0, j)),
            pl.BlockSpec((1, tn), lambda j: (0, j)),
        ],
        out_specs=pl.BlockSpec((r, tn), lambda j: (0, j)),
        out_shape=jax.ShapeDtypeStruct((r, n), F32),
        compiler_params=_cparams(("arbitrary",)),
        name="adaln",
    )(c_all, w_ada, b_ada.reshape(1, n))


def _proj_in_kernel(x_ref, sh_ref, sc_ref, g1_ref, w_ref, cos_ref, sin_ref, gq_ref, gk_ref, gv_ref,
                    seg64_ref, seg128_ref,
                    q_ref, k_ref, kb_ref, v_ref, vb_ref, u_ref, vg_ref, gate_ref):
    x = x_ref[...]
    ms = jnp.mean(x * x, axis=-1, keepdims=True)
    h = x * lax.rsqrt(ms + EPS) * g1_ref[...]
    h = h * (1.0 + sc_ref[0]) + sh_ref[0]
    hb = h.astype(BF)
    cos = cos_ref[...]
    sin = sin_ref[...]
    seg64 = seg64_ref[...]
    seg128 = seg128_ref[...]
    tm = x.shape[0]
    cw = 256
    lane = lax.broadcasted_iota(I32, (tm, cw), 1)
    first_half = (lane % ATT_DH) < (ATT_DH // 2)

    def qk_tile(col, g):
        z = jnp.dot(hb, w_ref[:, col:col + cw], preferred_element_type=F32)
        y = z * lax.rsqrt(_seg_mean_sq(z, seg64, ATT_DH) + EPS) * g
        rot = jnp.where(first_half, pltpu.roll(y, cw - ATT_DH // 2, 1), pltpu.roll(y, ATT_DH // 2, 1))
        return y * cos + rot * sin

    for t in range(D_MODEL // cw):
        c = t * cw
        q = qk_tile(c, gq_ref[...])
        q_ref[:, c:c + cw] = (q * (ATT_DH ** -0.5)).astype(BF)
        k = qk_tile(D_MODEL + c, gk_ref[...])
        k_ref[:, c:c + cw] = k
        kb_ref[:, c:c + cw] = k.astype(BF)
        v = jnp.dot(hb, w_ref[:, 2 * D_MODEL + c:2 * D_MODEL + c + cw], preferred_element_type=F32)
        v_ref[:, c:c + cw] = v
        vb_ref[:, c:c + cw] = v.astype(BF)
        zu = jnp.dot(hb, w_ref[:, 3 * D_MODEL + c:3 * D_MODEL + c + cw], preferred_element_type=F32)
        u_ref[:, c:c + cw] = _gelu(zu).astype(BF)
        zv = jnp.dot(hb, w_ref[:, 4 * D_MODEL + c:4 * D_MODEL + c + cw], preferred_element_type=F32)
        gv = _gelu(zv)
        vg = gv * lax.rsqrt(_seg_mean_sq(gv, seg128, GM_WIDTH // GM_GROUPS) + EPS) * gv_ref[...]
        vg_ref[:, c:c + cw] = vg.astype(vg_ref.dtype)
    for t in range(2 * D_MODEL // cw):
        c = t * cw
        zg = jnp.dot(hb, w_ref[:, 5 * D_MODEL + c:5 * D_MODEL + c + cw], preferred_element_type=F32)
        gate_ref[:, c:c + cw] = jax.nn.sigmoid(zg).astype(BF)


def _proj_in(x2, sh, sc, g1, w_in_b, cos, sin, gq, gk, gv, seg64, seg128, tm, rows_per_mod, vg_dtype):
    r = x2.shape[0]
    rg = sh.shape[1]
    mod_spec = pl.BlockSpec((1, rg, D_MODEL), lambda i: ((i * tm) // rows_per_mod, 0, 0))
    row = lambda w: pl.BlockSpec((tm, w), lambda i: (i, 0))
    full = lambda a: pl.BlockSpec(a.shape, lambda i: (0,) * a.ndim, pipeline_mode=pl.Buffered(1))
    out_shapes = [
        jax.ShapeDtypeStruct((r, D_MODEL), BF),
        jax.ShapeDtypeStruct((r, D_MODEL), F32),
        jax.ShapeDtypeStruct((r, D_MODEL), BF),
        jax.ShapeDtypeStruct((r, D_MODEL), F32),
        jax.ShapeDtypeStruct((r, D_MODEL), BF),
        jax.ShapeDtypeStruct((r, D_MODEL), BF),
        jax.ShapeDtypeStruct((r, D_MODEL), vg_dtype),
        jax.ShapeDtypeStruct((r, 2 * D_MODEL), BF),
    ]
    return pl.pallas_call(
        _proj_in_kernel,
        grid=(r // tm,),
        in_specs=[row(D_MODEL), mod_spec, mod_spec, full(g1), full(w_in_b), row(256), row(256),
                  full(gq), full(gk), full(gv), full(seg64), full(seg128)],
        out_specs=[row(D_MODEL)] * 7 + [row(2 * D_MODEL)],
        out_shape=out_shapes,
        compiler_params=_cparams(("arbitrary",)),
        name="proj_in",
    )(x2, sh, sc, g1, w_in_b, cos, sin, gq, gk, gv, seg64, seg128)


def _lambda_value(lam_ref, lam_init):
    lp = lam_ref[...]
    a = jnp.sum(lp[0:1] * lp[1:2], axis=-1, keepdims=True)
    b = jnp.sum(lp[2:3] * lp[3:4], axis=-1, keepdims=True)
    return jnp.exp(a) - jnp.exp(b) + lam_init


def _subln(o, sg, lam_init):
    ss = jnp.mean(o * o, axis=-1, keepdims=True)
    return o * lax.rsqrt(ss + EPS) * sg * (1.0 - lam_init)


def _attn_kernel(lam_ref, q_ref, k_ref, v_ref, sg_ref, o_ref, m_sc, l_sc, acc_sc, *, tq, lam_init):
    qi = pl.program_id(2)
    q = q_ref[...]
    lane = lax.broadcasted_iota(I32, q.shape, 1)
    zero = jnp.zeros_like(q)
    q2 = jnp.concatenate([jnp.where(lane < ATT_DH, q, zero), jnp.where(lane >= ATT_DH, q, zero)], axis=0)
    m_sc[...] = jnp.full(m_sc.shape, NEG_INF, F32)
    l_sc[...] = jnp.zeros(l_sc.shape, F32)
    acc_sc[...] = jnp.zeros(acc_sc.shape, F32)

    def step(j, masked):
        start = pl.multiple_of(j * tq, tq)
        k = k_ref[pl.ds(start, tq), :]
        v = v_ref[pl.ds(start, tq), :]
        s = lax.dot_general(q2, k, (((1,), (1,)), ((), ())), preferred_element_type=F32)
        if masked:
            r = lax.broadcasted_iota(I32, s.shape, 0)
            c = lax.broadcasted_iota(I32, s.shape, 1)
            qpos = jnp.where(r >= tq, r - tq, r)
            s = jnp.where(c <= qpos, s, NEG_INF)
        m_prev = m_sc[...]
        m_new = jnp.maximum(m_prev, jnp.max(s, axis=-1, keepdims=True))
        alpha = jnp.exp(m_prev - m_new)
        p = jnp.exp(s - m_new)
        l_sc[...] = alpha * l_sc[...] + jnp.sum(p, axis=-1, keepdims=True)
        acc_sc[...] = alpha * acc_sc[...] + jnp.dot(p.astype(BF), v, preferred_element_type=F32)
        m_sc[...] = m_new

    def body(j, carry):
        step(j, False)
        return carry

    lax.fori_loop(0, qi, body, 0)
    step(qi, True)

    lam = _lambda_value(lam_ref, lam_init)
    o_all = acc_sc[...] / l_sc[...]
    o = o_all[:tq] - lam * o_all[tq:]
    o_ref[...] = _subln(o, sg_ref[...], lam_init).astype(o_ref.dtype)


def _prompt_attention(lam_p, qb, kb, vb, sg, lam_init, tq=512):
    b, t, _ = qb.shape
    kern = functools.partial(_attn_kernel, tq=tq, lam_init=lam_init)
    return pl.pallas_call(
        kern,
        grid=(b, ATT_HEADS, t // tq),
        in_specs=[
            pl.BlockSpec(lam_p.shape, lambda bi, h, i: (0, 0)),
            pl.BlockSpec((None, tq, ATT_VD), lambda bi, h, i: (bi, i, h)),
            pl.BlockSpec((None, t, ATT_VD), lambda bi, h, i: (bi, 0, h)),
            pl.BlockSpec((None, t, ATT_VD), lambda bi, h, i: (bi, 0, h)),
            pl.BlockSpec(sg.shape, lambda bi, h, i: (0, 0)),
        ],
        out_specs=pl.BlockSpec((None, tq, ATT_VD), lambda bi, h, i: (bi, i, h)),
        out_shape=jax.ShapeDtypeStruct((b, t, ATT_HEADS * ATT_VD), BF),
        scratch_shapes=[pltpu.VMEM((2 * tq, 1), F32), pltpu.VMEM((2 * tq, 1), F32),
                        pltpu.VMEM((2 * tq, ATT_VD), F32)],
        compiler_params=_cparams(("arbitrary", "arbitrary", "arbitrary")),
        name="prompt_attn",
    )(lam_p, qb, kb, vb, sg)


DEC_PP = 4
DEC_COLS = 64


def _dec_attn_kernel(pt_ref, lam_ref, qbd_ref, kn_ref, vn_ref, sg_ref, *rest, n_q, lam_init):
    k_refs = rest[:DEC_PP]
    v_refs = rest[DEC_PP:2 * DEC_PP]
    o_ref = rest[2 * DEC_PP]
    m_sc, l_sc, acc_sc = rest[2 * DEC_PP + 1:]
    g = pl.program_id(1)
    qbd = qbd_ref[0]

    @pl.when(g == 0)
    def _():
        m_sc[...] = jnp.full(m_sc.shape, NEG_INF, F32)
        l_sc[...] = jnp.zeros(l_sc.shape, F32)
        acc_sc[...] = jnp.zeros(acc_sc.shape, F32)

    def update(s_list, v_list):
        m_prev = m_sc[...]
        m_new = m_prev
        for s in s_list:
            m_new = jnp.maximum(m_new, jnp.max(s, axis=-1, keepdims=True))
        alpha = jnp.exp(m_prev - m_new)
        l_new = alpha * l_sc[...]
        acc = alpha * acc_sc[...]
        for s, v in zip(s_list, v_list):
            p = jnp.exp(s - m_new)
            l_new = l_new + jnp.sum(p, axis=-1, keepdims=True)
            acc = acc + jnp.dot(p.astype(BF), v, preferred_element_type=F32)
        m_sc[...] = m_new
        l_sc[...] = l_new
        acc_sc[...] = acc

    s_list, v_list = [], []
    for j in range(DEC_PP):
        kp = k_refs[j][0].astype(BF)
        s_list.append(lax.dot_general(qbd, kp, (((1,), (1,)), ((), ())), preferred_element_type=F32))
        v_list.append(v_refs[j][0].astype(BF))
    update(s_list, v_list)

    @pl.when(g == pl.num_programs(1) - 1)
    def _():
        kn = kn_ref[0].astype(BF)
        s = lax.dot_general(qbd, kn, (((1,), (1,)), ((), ())), preferred_element_type=F32)
        r = lax.broadcasted_iota(I32, s.shape, 0)
        c = lax.broadcasted_iota(I32, s.shape, 1)
        s = jnp.where(c <= (r % n_q), s, NEG_INF)
        update([s], [vn_ref[0].astype(BF)])
        lam = _lambda_value(lam_ref, lam_init)
        o_all = acc_sc[...] / l_sc[...]
        for h in range(ATT_HEADS):
            blk = o_all[h * 2 * n_q:(h + 1) * 2 * n_q, h * ATT_VD:(h + 1) * ATT_VD]
            o = blk[:n_q] - lam * blk[n_q:]
            o_ref[0, :, h * ATT_VD:(h + 1) * ATT_VD] = _subln(o, sg_ref[...], lam_init).astype(o_ref.dtype)


def _sample_attention(page_table, lam_p, qbd, kn, vn, sg, cache_k2, cache_v2, n_q, lam_init):
    nb, n_pages = page_table.shape
    page = cache_k2.shape[1]
    width = cache_k2.shape[2]

    def page_spec(j):
        return pl.BlockSpec((1, page, width), lambda b, g, pt: (pt[b, g * DEC_PP + j], 0, 0))

    kern = functools.partial(_dec_attn_kernel, n_q=n_q, lam_init=lam_init)
    grid_spec = pltpu.PrefetchScalarGridSpec(
        num_scalar_prefetch=1,
        grid=(nb, n_pages // DEC_PP),
        in_specs=[
            pl.BlockSpec(lam_p.shape, lambda b, g, pt: (0, 0)),
            pl.BlockSpec((1, DEC_COLS, width), lambda b, g, pt: (b, 0, 0)),
            pl.BlockSpec((1, 8, width), lambda b, g, pt: (b, 0, 0)),
            pl.BlockSpec((1, 8, width), lambda b, g, pt: (b, 0, 0)),
            pl.BlockSpec(sg.shape, lambda b, g, pt: (0, 0)),
        ] + [page_spec(j) for j in range(DEC_PP)] + [page_spec(j) for j in range(DEC_PP)],
        out_specs=pl.BlockSpec((1, n_q, width), lambda b, g, pt: (b, 0, 0)),
        scratch_shapes=[pltpu.VMEM((DEC_COLS, 1), F32), pltpu.VMEM((DEC_COLS, 1), F32),
                        pltpu.VMEM((DEC_COLS, width), F32)],
    )
    return pl.pallas_call(
        kern,
        grid_spec=grid_spec,
        out_shape=jax.ShapeDtypeStruct((nb, n_q, width), BF),
        compiler_params=_cparams(("arbitrary", "arbitrary")),
        name="sample_attn",
    )(page_table, lam_p, qbd, kn, vn, sg, *([cache_k2] * DEC_PP), *([cache_v2] * DEC_PP))


def _mix_kernel(x_ref, a_ref, u_ref, vg_ref, gate_ref, g1_ref, sh2_ref, sc2_ref, n2_ref,
                ws_ref, wmask_ref, bias_ref, wa_ref, wb_ref, wo_ref, x1_ref, h2_ref):
    tm = x_ref.shape[0]
    gc = GM_WIDTH // GM_GROUPS
    wmask = wmask_ref[...]
    bias = bias_ref[...]
    s_chunks = []
    for c in range(tm // GM_CHUNK):
        rows = slice(c * GM_CHUNK, (c + 1) * GM_CHUNK)
        cols = []
        for g in range(GM_GROUPS):
            w = (ws_ref[g] * wmask).astype(BF)
            vv = vg_ref[rows, g * gc:(g + 1) * gc].astype(BF)
            cols.append(jnp.dot(w, vv, preferred_element_type=F32))
        mixed = jnp.concatenate(cols, axis=1) + bias
        s_chunks.append((u_ref[rows, :].astype(F32) * mixed).astype(BF))
    s_out = jnp.concatenate(s_chunks, axis=0) if len(s_chunks) > 1 else s_chunks[0]
    ya = jnp.dot(a_ref[...], wa_ref[...], preferred_element_type=F32)
    yb = jnp.dot(s_out, wb_ref[...], preferred_element_type=F32)
    ga = gate_ref[:, :D_MODEL].astype(F32)
    gb = gate_ref[:, D_MODEL:].astype(F32)
    mix = (ga * ya + gb * yb).astype(BF)
    y = jnp.dot(mix, wo_ref[...], preferred_element_type=F32)
    x1 = x_ref[...] + g1_ref[0] * y
    x1_ref[...] = x1
    ms = jnp.mean(x1 * x1, axis=-1, keepdims=True)
    h2 = x1 * lax.rsqrt(ms + EPS) * n2_ref[...]
    h2 = h2 * (1.0 + sc2_ref[0]) + sh2_ref[0]
    h2_ref[...] = h2.astype(BF)


def _mix(x2, a_out, u, vg, gates, g1, sh2, sc2, n2, ws_exp, wmask, bias_full, wa, wb, wo, tm, rows_per_mod):
    r = x2.shape[0]
    rg = g1.shape[1]
    mod_spec = pl.BlockSpec((1, rg, D_MODEL), lambda i: ((i * tm) // rows_per_mod, 0, 0))
    row = lambda w: pl.BlockSpec((tm, w), lambda i: (i, 0))
    full = lambda a: pl.BlockSpec(a.shape, lambda i: (0,) * a.ndim)
    return pl.pallas_call(
        _mix_kernel,
        grid=(r // tm,),
        in_specs=[row(D_MODEL), row(D_MODEL), row(D_MODEL), row(D_MODEL), row(2 * D_MODEL),
                  mod_spec, mod_spec, mod_spec, full(n2), full(ws_exp), full(wmask), full(bias_full),
                  full(wa), full(wb), full(wo)],
        out_specs=[row(D_MODEL), row(D_MODEL)],
        out_shape=[jax.ShapeDtypeStruct((r, D_MODEL), F32), jax.ShapeDtypeStruct((r, D_MODEL), BF)],
        compiler_params=_cparams(("arbitrary",)),
        name="mix",
    )(x2, a_out, u, vg, gates, g1, sh2, sc2, n2, ws_exp, wmask, bias_full, wa, wb, wo)


def _top_rows(s, iota, n_rows, count):
    vals, idxs = [], []
    for _ in range(count):
        m = jnp.max(s, axis=0, keepdims=True)
        idx = jnp.min(jnp.where(s == m, iota, n_rows), axis=0, keepdims=True)
        vals.append(m)
        idxs.append(idx)
        s = jnp.where(iota == idx, NEG_INF, s)
    return vals, idxs


def _peer_topk_kernel(h2_ref, wq_ref, sk_ref, i1_ref, i2_ref, g_ref, qT_sc, val_sc, idx_sc):
    tm = h2_ref.shape[0]
    nk = PEER_NKEYS
    qT_sc[...] = lax.dot_general(wq_ref[...], h2_ref[...], (((1,), (1,)), ((), ())),
                                 preferred_element_type=F32).astype(BF)
    iota_k = lax.broadcasted_iota(I32, (nk, tm), 0)

    def half_body(hp, carry):
        start = pl.multiple_of(hp * nk, nk)
        s = jnp.dot(sk_ref[hp], qT_sc[pl.ds(start, nk), :], preferred_element_type=F32)
        vals, idxs = _top_rows(s, iota_k, nk, PEER_TOPK)
        val_sc[hp] = jnp.concatenate(vals, axis=0)
        idx_sc[hp] = jnp.concatenate(idxs, axis=0)
        return carry

    lax.fori_loop(0, 2 * PEER_HEADS, half_body, 0)

    nc = PEER_TOPK * PEER_TOPK
    iota_c = lax.broadcasted_iota(I32, (nc, tm), 0)

    def head_body(h, carry):
        v1 = val_sc[2 * h]
        v2 = val_sc[2 * h + 1]
        j1 = idx_sc[2 * h]
        j2 = idx_sc[2 * h + 1]
        cand = jnp.concatenate([v1[a:a + 1] + v2 for a in range(PEER_TOPK)], axis=0)
        cidx = jnp.concatenate([j1[a:a + 1] * nk + j2 for a in range(PEER_TOPK)], axis=0)
        tops, eids = [], []
        for _ in range(PEER_TOPK):
            m = jnp.max(cand, axis=0, keepdims=True)
            pos = jnp.min(jnp.where(cand == m, iota_c, nc), axis=0, keepdims=True)
            hit = iota_c == pos
            eids.append(jnp.max(jnp.where(hit, cidx, -1), axis=0, keepdims=True))
            tops.append(m)
            cand = jnp.where(hit, NEG_INF, cand)
        ts = jnp.concatenate(tops, axis=0)
        e = jnp.exp(ts - tops[0])
        gates = e / jnp.sum(e, axis=0, keepdims=True)
        eid = jnp.concatenate(eids, axis=0)
        row0 = pl.multiple_of(h * PEER_TOPK, PEER_TOPK)
        i1_ref[pl.ds(row0, PEER_TOPK), :] = lax.shift_right_logical(eid, 7)
        i2_ref[pl.ds(row0, PEER_TOPK), :] = lax.bitwise_and(eid, nk - 1)
        g_ref[pl.ds(row0, PEER_TOPK), :] = gates
        return carry

    lax.fori_loop(0, PEER_HEADS, head_body, 0)


def _peer_topk(h2, wqT, sk, tm):
    t = h2.shape[0]
    nq = wqT.shape[0]
    out = jax.ShapeDtypeStruct((PEER_SEL, t), I32)
    col = pl.BlockSpec((PEER_SEL, tm), lambda i: (0, i))
    return pl.pallas_call(
        _peer_topk_kernel,
        grid=(t // tm,),
        in_specs=[pl.BlockSpec((tm, D_MODEL), lambda i: (i, 0)),
                  pl.BlockSpec(wqT.shape, lambda i: (0, 0)),
                  pl.BlockSpec(sk.shape, lambda i: (0, 0, 0))],
        out_specs=[col, col, col],
        out_shape=[out, out, jax.ShapeDtypeStruct((PEER_SEL, t), F32)],
        scratch_shapes=[pltpu.VMEM((nq, tm), BF),
                        pltpu.VMEM((2 * PEER_HEADS, PEER_TOPK, tm), F32),
                        pltpu.VMEM((2 * PEER_HEADS, PEER_TOPK, tm), I32)],
        compiler_params=_cparams(("arbitrary",)),
        name="peer_topk",
    )(h2, wqT, sk)


def _peer_gates_kernel(i1_ref, i2_ref, g_ref, o_ref, scr):
    tm = i1_ref.shape[0]
    nk = PEER_NKEYS
    iota = lax.broadcasted_iota(I32, (nk, PEER_SEL), 0)

    def body(t, carry):
        r1 = i1_ref[pl.ds(t, 1), :]
        r2 = i2_ref[pl.ds(t, 1), :]
        gg = g_ref[pl.ds(t, 1), :]
        a = jnp.where(iota == r1, gg, 0.0).astype(BF)
        b = jnp.where(iota == r2, 1.0, 0.0).astype(BF)
        gt = lax.dot_general(a, b, (((1,), (1,)), ((), ())), preferred_element_type=F32)
        row0 = pl.multiple_of(t * nk, nk)
        scr[pl.ds(row0, nk), :] = gt
        return carry

    lax.fori_loop(0, tm, body, 0)
    for i in range(nk):
        o_ref[i] = scr[pl.ds(i, tm, stride=nk), :].astype(o_ref.dtype)


def _peer_gates(i1, i2, g, tm):
    t = i1.shape[0]
    row = pl.BlockSpec((tm, PEER_SEL), lambda i: (i, 0))
    return pl.pallas_call(
        _peer_gates_kernel,
        grid=(t // tm,),
        in_specs=[row, row, row],
        out_specs=pl.BlockSpec((PEER_NKEYS, tm, PEER_NKEYS), lambda i: (0, i, 0)),
        out_shape=jax.ShapeDtypeStruct((PEER_NKEYS, t, PEER_NKEYS), BF),
        scratch_shapes=[pltpu.VMEM((tm * PEER_NKEYS, PEER_NKEYS), F32)],
        compiler_params=_cparams(("arbitrary",)),
        name="peer_gates",
    )(i1, i2, g)


PEER_EB = 8


def _peer_dense_kernel(h2_ref, x1_ref, g2_ref, gm_ref, u_ref, v_ref, o_ref, acc_sc):
    e = pl.program_id(1)

    @pl.when(e == 0)
    def _():
        acc_sc[...] = jnp.zeros(acc_sc.shape, F32)

    h2 = h2_ref[...]
    nk = PEER_NKEYS
    ws = []
    for il in range(PEER_EB):
        a = lax.dot_general(h2, u_ref[il * nk:(il + 1) * nk, :], (((1,), (1,)), ((), ())),
                            preferred_element_type=F32)
        ws.append((gm_ref[il].astype(F32) * _gelu(a)).astype(BF))
    w = jnp.concatenate(ws, axis=1)
    acc_sc[...] += jnp.dot(w, v_ref[...], preferred_element_type=F32)

    @pl.when(e == pl.num_programs(1) - 1)
    def _():
        o_ref[...] = x1_ref[...] + g2_ref[0] * acc_sc[...]


def _peer_dense(h2, x1, g2, gmat, u_b, v_b, tm, rows_per_mod):
    t = h2.shape[0]
    ne = u_b.shape[0]
    eb = PEER_EB * PEER_NKEYS
    rg = g2.shape[1]
    return pl.pallas_call(
        _peer_dense_kernel,
        grid=(t // tm, ne // eb),
        in_specs=[pl.BlockSpec((tm, D_MODEL), lambda i, e: (i, 0)),
                  pl.BlockSpec((tm, D_MODEL), lambda i, e: (i, 0)),
                  pl.BlockSpec((1, rg, D_MODEL), lambda i, e: ((i * tm) // rows_per_mod, 0, 0)),
                  pl.BlockSpec((PEER_EB, tm, PEER_NKEYS), lambda i, e: (e, i, 0)),
                  pl.BlockSpec((eb, D_MODEL), lambda i, e: (e, 0)),
                  pl.BlockSpec((eb, D_MODEL), lambda i, e: (e, 0))],
        out_specs=pl.BlockSpec((tm, D_MODEL), lambda i, e: (i, 0)),
        out_shape=jax.ShapeDtypeStruct((t, D_MODEL), F32),
        scratch_shapes=[pltpu.VMEM((tm, D_MODEL), F32)],
        compiler_params=_cparams(("arbitrary", "arbitrary")),
        name="peer_dense",
    )(h2, x1, g2, gmat, u_b, v_b)


def _rope_tables(pos, width=256):
    inv = 1.0 / (ROPE_THETA ** (jnp.arange(0, ATT_DH, 2, dtype=F32) / ATT_DH))
    ang = pos.astype(F32)[:, None] * inv[None, :]
    cos = jnp.cos(ang)
    sin = jnp.sin(ang)
    reps = width // ATT_DH
    cos_t = jnp.tile(jnp.concatenate([cos, cos], axis=1), (1, reps))
    sin_t = jnp.tile(jnp.concatenate([-sin, sin], axis=1), (1, reps))
    return cos_t, sin_t


def _block_diag_ones(width, seg):
    r = jnp.arange(width) // seg
    return (r[:, None] == r[None, :]).astype(BF)


def _peer(h2, x1, g2, wqT, sk, u_b, v_b, tm_topk, tm_gates, tm_dense, rows_per_mod):
    i1t, i2t, gt = _peer_topk(h2, wqT, sk, tm_topk)
    gmat = _peer_gates(i1t.T, i2t.T, gt.T, tm_gates)
    return _peer_dense(h2, x1, g2, gmat, u_b, v_b, tm_dense, rows_per_mod)


def kernel(x_prompt, x_sample, cache_k, cache_v, page_table, c_prompt, c_sample, w_ada, b_ada, norm1_g, w_in, q_norm_g, k_norm_g, lambda_q1, lambda_k1, lambda_q2, lambda_k2, subln_g, gm_v_norm_g, gm_ws, gm_bs, w_att_out, w_gm_out, w_out, norm2_g, peer_w_q, peer_subkeys, peer_u, peer_v):
    depth = w_ada.shape[0]
    assert depth == 1
    l = 0
    lam_init = 0.8 - 0.6 * math.exp(-0.3 * l)
    bsz, t_p, d = x_prompt.shape
    nb, t_s, _ = x_sample.shape
    n_pool, page = cache_k.shape[1], cache_k.shape[2]
    past_len = page_table.shape[1] * page

    w_in_b = w_in[l].astype(BF)
    wa_b = w_att_out[l].astype(BF)
    wb_b = w_gm_out[l].astype(BF)
    wo_b = w_out[l].astype(BF)
    wqT_b = peer_w_q[l].T.astype(BF)
    sk_b = peer_subkeys[l].reshape(2 * PEER_HEADS, PEER_NKEYS, -1).astype(BF)
    u_b = peer_u[l].astype(BF)
    v_b = peer_v[l].astype(BF)
    g1n = norm1_g[l].reshape(1, d)
    g2n = norm2_g[l].reshape(1, d)
    gq = jnp.tile(q_norm_g[l], 256 // ATT_DH).reshape(1, 256)
    gk = jnp.tile(k_norm_g[l], 256 // ATT_DH).reshape(1, 256)
    gv = jnp.tile(gm_v_norm_g[l], 2).reshape(1, 256)
    sg = subln_g[l].reshape(1, ATT_VD)
    lam_p = jnp.stack([lambda_q1[l], lambda_k1[l], lambda_q2[l], lambda_k2[l]], axis=0)
    seg64 = _block_diag_ones(256, ATT_DH)
    seg128 = _block_diag_ones(256, GM_WIDTH // GM_GROUPS)
    bias_p = jnp.repeat(gm_bs[l].T, GM_WIDTH // GM_GROUPS, axis=1)
    ri = jnp.arange(GM_CHUNK)
    tril_p = (ri[None, :] <= ri[:, None]).astype(F32)
    ws_s = jnp.tile(gm_ws[l][:, :t_s, :t_s], (1, GM_CHUNK // t_s, GM_CHUNK // t_s))
    mask_s = ((ri[:, None] // t_s == ri[None, :] // t_s) & (ri[None, :] % t_s <= ri[:, None] % t_s)).astype(F32)
    bias_s = jnp.tile(bias_p[:t_s], (GM_CHUNK // t_s, 1))

    n_c = bsz + nb
    c_all = jnp.concatenate([c_prompt, c_sample, jnp.zeros((-n_c % 8, d), F32)], axis=0)
    mod = _adaln(c_all, w_ada[l], b_ada[l])
    mod_p = mod[:bsz].reshape(bsz, 1, 6, d)
    mod_s = jnp.repeat(mod[bsz:n_c], t_s, axis=0).reshape(1, nb * t_s, 6, d)
    mp = [mod_p[:, :, i, :] for i in range(6)]
    ms_ = [mod_s[:, :, i, :] for i in range(6)]

    xp2 = x_prompt.reshape(bsz * t_p, d)
    cos_p, sin_p = _rope_tables(jnp.arange(t_p))
    cos_pt = jnp.tile(cos_p, (bsz, 1))
    sin_pt = jnp.tile(sin_p, (bsz, 1))
    q_p, k_p, kb_p, v_p, vb_p, u_p, vg_p, gates_p = _proj_in(
        xp2, mp[0], mp[1], g1n, w_in_b, cos_pt, sin_pt, gq, gk, gv, seg64, seg128,
        tm=512, rows_per_mod=t_p, vg_dtype=BF)
    a_p = _prompt_attention(lam_p, q_p.reshape(bsz, t_p, d), kb_p.reshape(bsz, t_p, d),
                            vb_p.reshape(bsz, t_p, d), sg, lam_init)
    x1_p, h2_p = _mix(xp2, a_p.reshape(bsz * t_p, d), u_p, vg_p, gates_p, mp[2], mp[3], mp[4], g2n,
                      gm_ws[l], tril_p, bias_p, wa_b, wb_b, wo_b, tm=256, rows_per_mod=t_p)
    y_p = _peer(h2_p, x1_p, mp[5], wqT_b, sk_b, u_b, v_b,
                tm_topk=256, tm_gates=64, tm_dense=1024, rows_per_mod=t_p)

    r_s = nb * t_s
    xs2 = x_sample.reshape(r_s, d)
    cos_s, sin_s = _rope_tables(past_len + jnp.arange(t_s))
    cos_st = jnp.tile(cos_s, (nb, 1))
    sin_st = jnp.tile(sin_s, (nb, 1))
    q_s, k_s, _, v_s, _, u_s, vg_s, gates_s = _proj_in(
        xs2, ms_[0], ms_[1], g1n, w_in_b, cos_st, sin_st, gq, gk, gv, seg64, seg128,
        tm=r_s, rows_per_mod=r_s, vg_dtype=F32)
    q3 = q_s.reshape(nb, t_s, d)
    q_rep = jnp.tile(q3, (1, DEC_COLS // t_s, 1))
    rr = jnp.arange(DEC_COLS)[:, None] // t_s
    cc = jnp.arange(d)[None, :] // ATT_DH
    qbd = jnp.where(rr == cc, q_rep, jnp.zeros_like(q_rep))
    pad8 = lambda a: jnp.pad(a.reshape(nb, t_s, d), ((0, 0), (0, 8 - t_s), (0, 0)))
    a_s = _sample_attention(page_table, lam_p, qbd, pad8(k_s), pad8(v_s), sg,
                            cache_k[l].reshape(n_pool, page, d), cache_v[l].reshape(n_pool, page, d),
                            n_q=t_s, lam_init=lam_init)
    x1_s, h2_s = _mix(xs2, a_s.reshape(r_s, d), u_s, vg_s, gates_s, ms_[2], ms_[3], ms_[4], g2n,
                      ws_s, mask_s, bias_s, wa_b, wb_b, wo_b, tm=r_s, rows_per_mod=r_s)
    y_s = _peer(h2_s, x1_s, ms_[5], wqT_b, sk_b, u_b, v_b,
                tm_topk=r_s, tm_gates=64, tm_dense=r_s, rows_per_mod=r_s)

    return (
        y_p.reshape(bsz, t_p, d),
        y_s.reshape(nb, t_s, d),
        k_p.reshape(1, bsz, t_p, ATT_HEADS, 2, ATT_DH),
        v_p.reshape(1, bsz, t_p, ATT_HEADS, ATT_VD),
        k_s.reshape(1, nb, t_s, ATT_HEADS, 2, ATT_DH),
        v_s.reshape(1, nb, t_s, ATT_HEADS, ATT_VD),
        vg_s.reshape(1, nb, t_s, GM_WIDTH),
    )
```

```python
import functools
import math

import jax
import jax.numpy as jnp
from jax import lax
from jax.experimental import pallas as pl
from jax.experimental.pallas import tpu as pltpu

BF = jnp.bfloat16
F32 = jnp.float32
I32 = jnp.int32

D_MODEL = 1024
ATT_HEADS = 8
ATT_DH = 64
ATT_VD = 128
ROPE_THETA = 10000.0
GM_GROUPS = 8
GM_CHUNK = 128
GM_WIDTH = 1024
IN_WIDTH = 7168
PEER_HEADS = 8
PEER_NKEYS = 128
PEER_TOPK = 16
PEER_SEL = PEER_HEADS * PEER_TOPK
EPS = 1e-6
NEG_INF = float("-inf")

VMEM_LIMIT = 56 * 1024 * 1024


def _cparams(sem):
    return pltpu.CompilerParams(dimension_semantics=sem, vmem_limit_bytes=VMEM_LIMIT)


def _gelu(x):
    return 0.5 * x * (1.0 + lax.erf(x * math.sqrt(0.5)))


def _seg_mean_sq(z, seg, n):
    z2 = z * z
    hi = z2.astype(BF)
    lo = (z2 - hi.astype(F32)).astype(BF)
    ss = jnp.dot(hi, seg, preferred_element_type=F32) + jnp.dot(lo, seg, preferred_element_type=F32)
    return ss * (1.0 / n)


def _ada_kernel(c_ref, w_ref, b_ref, o_ref):
    c = c_ref[...]
    s = c * jax.nn.sigmoid(c)
    o_ref[...] = jnp.dot(s.astype(BF), w_ref[...].astype(BF), preferred_element_type=F32) + b_ref[...]


def _adaln(c_all, w_ada, b_ada):
    r = c_all.shape[0]
    n = w_ada.shape[1]
    tn = 1024
    return pl.pallas_call(
        _ada_kernel,
        grid=(n // tn,),
        in_specs=[
            pl.BlockSpec((r, D_MODEL), lambda j: (0, 0)),
            pl.BlockSpec((D_MODEL, tn), lambda j: (0, j)),
            pl.BlockSpec((1, tn), lambda j: (0, j)),
        ],
        out_specs=pl.BlockSpec((r, tn), lambda j: (0, j)),
        out_shape=jax.ShapeDtypeStruct((r, n), F32),
        compiler_params=_cparams(("arbitrary",)),
        name="adaln",
    )(c_all, w_ada, b_ada.reshape(1, n))


def _proj_in_kernel(x_ref, sh_ref, sc_ref, g1_ref, w_ref, cos_ref, sin_ref, gq_ref, gk_ref, gv_ref,
                    seg64_ref, seg128_ref,
                    q_ref, k_ref, kb_ref, v_ref, vb_ref, u_ref, vg_ref, gate_ref, *, k_transposed):
    x = x_ref[...]
    ms = jnp.mean(x * x, axis=-1, keepdims=True)
    h = x * lax.rsqrt(ms + EPS) * g1_ref[...]
    h = h * (1.0 + sc_ref[0]) + sh_ref[0]
    hb = h.astype(BF)
    cos = cos_ref[...]
    sin = sin_ref[...]
    seg64 = seg64_ref[...]
    seg128 = seg128_ref[...]
    tm = x.shape[0]
    cw = 256
    lane = lax.broadcasted_iota(I32, (tm, cw), 1)
    first_half = (lane % ATT_DH) < (ATT_DH // 2)

    def qk_tile(col, g):
        z = jnp.dot(hb, w_ref[:, col:col + cw], preferred_element_type=F32)
        y = z * lax.rsqrt(_seg_mean_sq(z, seg64, ATT_DH) + EPS) * g
        rot = jnp.where(first_half, pltpu.roll(y, cw - ATT_DH // 2, 1), pltpu.roll(y, ATT_DH // 2, 1))
        return y * cos + rot * sin

    for t in range(D_MODEL // cw):
        c = t * cw
        q = qk_tile(c, gq_ref[...])
        q_ref[:, c:c + cw] = (q * (ATT_DH ** -0.5)).astype(BF)
        k = qk_tile(D_MODEL + c, gk_ref[...])
        if k_transposed:
            kt = k.T
            k_ref[c:c + cw, :] = kt
            kb_ref[c:c + cw, :] = kt.astype(BF)
        else:
            k_ref[:, c:c + cw] = k
            kb_ref[:, c:c + cw] = k.astype(BF)
        v = jnp.dot(hb, w_ref[:, 2 * D_MODEL + c:2 * D_MODEL + c + cw], preferred_element_type=F32)
        v_ref[:, c:c + cw] = v
        vb_ref[:, c:c + cw] = v.astype(BF)
        zu = jnp.dot(hb, w_ref[:, 3 * D_MODEL + c:3 * D_MODEL + c + cw], preferred_element_type=F32)
        u_ref[:, c:c + cw] = _gelu(zu).astype(BF)
        zv = jnp.dot(hb, w_ref[:, 4 * D_MODEL + c:4 * D_MODEL + c + cw], preferred_element_type=F32)
        gv = _gelu(zv)
        vg = gv * lax.rsqrt(_seg_mean_sq(gv, seg128, GM_WIDTH // GM_GROUPS) + EPS) * gv_ref[...]
        vg_ref[:, c:c + cw] = vg.astype(vg_ref.dtype)
    for t in range(2 * D_MODEL // cw):
        c = t * cw
        zg = jnp.dot(hb, w_ref[:, 5 * D_MODEL + c:5 * D_MODEL + c + cw], preferred_element_type=F32)
        gate_ref[:, c:c + cw] = jax.nn.sigmoid(zg).astype(BF)


def _proj_in(x2, sh, sc, g1, w_in_b, cos, sin, gq, gk, gv, seg64, seg128, tm, rows_per_mod, vg_dtype,
             k_transposed):
    r = x2.shape[0]
    rg = sh.shape[1]
    mod_spec = pl.BlockSpec((1, rg, D_MODEL), lambda i: ((i * tm) // rows_per_mod, 0, 0))
    row = lambda w: pl.BlockSpec((tm, w), lambda i: (i, 0))
    full = lambda a: pl.BlockSpec(a.shape, lambda i: (0,) * a.ndim, pipeline_mode=pl.Buffered(1))
    if k_transposed:
        tiles = rows_per_mod // tm
        k_shape = (r // rows_per_mod, D_MODEL, rows_per_mod)
        k_spec = pl.BlockSpec((None, D_MODEL, tm), lambda i: (i // tiles, 0, i % tiles))
    else:
        k_shape = (r, D_MODEL)
        k_spec = row(D_MODEL)
    out_shapes = [
        jax.ShapeDtypeStruct((r, D_MODEL), BF),
        jax.ShapeDtypeStruct(k_shape, F32),
        jax.ShapeDtypeStruct(k_shape, BF),
        jax.ShapeDtypeStruct((r, D_MODEL), F32),
        jax.ShapeDtypeStruct((r, D_MODEL), BF),
        jax.ShapeDtypeStruct((r, D_MODEL), BF),
        jax.ShapeDtypeStruct((r, D_MODEL), vg_dtype),
        jax.ShapeDtypeStruct((r, 2 * D_MODEL), BF),
    ]
    return pl.pallas_call(
        functools.partial(_proj_in_kernel, k_transposed=k_transposed),
        grid=(r // tm,),
        in_specs=[row(D_MODEL), mod_spec, mod_spec, full(g1), full(w_in_b), row(256), row(256),
                  full(gq), full(gk), full(gv), full(seg64), full(seg128)],
        out_specs=[row(D_MODEL), k_spec, k_spec] + [row(D_MODEL)] * 4 + [row(2 * D_MODEL)],
        out_shape=out_shapes,
        compiler_params=_cparams(("arbitrary",)),
        name="proj_in",
    )(x2, sh, sc, g1, w_in_b, cos, sin, gq, gk, gv, seg64, seg128)


def _lambda_value(lam_ref, lam_init):
    lp = lam_ref[...]
    a = jnp.sum(lp[0:1] * lp[1:2], axis=-1, keepdims=True)
    b = jnp.sum(lp[2:3] * lp[3:4], axis=-1, keepdims=True)
    return jnp.exp(a) - jnp.exp(b) + lam_init


def _subln(o, sg, lam_init):
    ss = jnp.mean(o * o, axis=-1, keepdims=True)
    return o * lax.rsqrt(ss + EPS) * sg * (1.0 - lam_init)


def _attn_kernel(lam_ref, q_ref, k_ref, v_ref, sg_ref, o_ref, vx_sc, m_sc, acc_sc, *, tq, lam_init):
    qi = pl.program_id(2)
    t = v_ref.shape[0]

    @pl.when(qi == 0)
    def _():
        vx_sc[:, :ATT_VD] = v_ref[...]
        vx_sc[:, ATT_VD:] = jnp.ones((t, ATT_VD), BF)

    q = q_ref[...]
    lane = lax.broadcasted_iota(I32, q.shape, 1)
    zero = jnp.zeros_like(q)
    q2 = jnp.concatenate([jnp.where(lane < ATT_DH, q, zero), jnp.where(lane >= ATT_DH, q, zero)], axis=0)
    m_sc[...] = jnp.full(m_sc.shape, NEG_INF, F32)
    acc_sc[...] = jnp.zeros(acc_sc.shape, F32)

    def step(j, masked):
        start = pl.multiple_of(j * tq, tq)
        s = jnp.dot(q2, k_ref[:, pl.ds(start, tq)], preferred_element_type=F32)
        if masked:
            r = lax.broadcasted_iota(I32, s.shape, 0)
            c = lax.broadcasted_iota(I32, s.shape, 1)
            qpos = jnp.where(r >= tq, r - tq, r)
            s = jnp.where(c <= qpos, s, NEG_INF)
        m_prev = m_sc[...]
        m_new = jnp.maximum(m_prev, jnp.max(s, axis=-1, keepdims=True))
        alpha = jnp.exp(m_prev - m_new)
        p = jnp.exp(s - jnp.tile(m_new, (1, tq // ATT_VD)))
        pv = jnp.dot(p.astype(BF), vx_sc[pl.ds(start, tq), :], preferred_element_type=F32)
        acc_sc[...] = jnp.tile(alpha, (1, 2)) * acc_sc[...] + pv
        m_sc[...] = m_new

    def body(j, carry):
        step(j, False)
        return carry

    lax.fori_loop(0, qi, body, 0)
    step(qi, True)

    lam = _lambda_value(lam_ref, lam_init)
    o_all = acc_sc[:, :ATT_VD] / acc_sc[:, ATT_VD:]
    o = o_all[:tq] - lam * o_all[tq:]
    o_ref[...] = _subln(o, sg_ref[...], lam_init).astype(o_ref.dtype)


def _prompt_attention(lam_p, qb, kbt, vb, sg, lam_init, tq=512):
    b, t, _ = qb.shape
    kern = functools.partial(_attn_kernel, tq=tq, lam_init=lam_init)
    return pl.pallas_call(
        kern,
        grid=(b, ATT_HEADS, t // tq),
        in_specs=[
            pl.BlockSpec(lam_p.shape, lambda bi, h, i: (0, 0)),
            pl.BlockSpec((None, tq, ATT_VD), lambda bi, h, i: (bi, i, h)),
            pl.BlockSpec((None, ATT_VD, t), lambda bi, h, i: (bi, h, 0)),
            pl.BlockSpec((None, t, ATT_VD), lambda bi, h, i: (bi, 0, h)),
            pl.BlockSpec(sg.shape, lambda bi, h, i: (0, 0)),
        ],
        out_specs=pl.BlockSpec((None, tq, ATT_VD), lambda bi, h, i: (bi, i, h)),
        out_shape=jax.ShapeDtypeStruct((b, t, ATT_HEADS * ATT_VD), BF),
        scratch_shapes=[pltpu.VMEM((t, 2 * ATT_VD), BF), pltpu.VMEM((2 * tq, ATT_VD), F32),
                        pltpu.VMEM((2 * tq, 2 * ATT_VD), F32)],
        compiler_params=_cparams(("arbitrary", "arbitrary", "arbitrary")),
        name="prompt_attn",
    )(lam_p, qb, kbt, vb, sg)


DEC_PP = 4
DEC_COLS = 64


def _dec_attn_kernel(pt_ref, lam_ref, qbd_ref, kn_ref, vn_ref, sg_ref, exp_ref, hm_ref, *rest, n_q, lam_init):
    k_refs = rest[:DEC_PP]
    v_refs = rest[DEC_PP:2 * DEC_PP]
    o_ref = rest[2 * DEC_PP]
    m_sc, l_sc, acc_sc = rest[2 * DEC_PP + 1:]
    g = pl.program_id(1)
    qbd = qbd_ref[0]
    rows = lax.broadcasted_iota(I32, (DEC_COLS, ATT_VD), 0)

    @pl.when(g == 0)
    def _():
        m_sc[...] = jnp.full(m_sc.shape, NEG_INF, F32)
        l_sc[...] = jnp.zeros(l_sc.shape, F32)
        acc_sc[...] = jnp.zeros(acc_sc.shape, F32)

    def rescale(s_list):
        m_prev = m_sc[...]
        m_new = m_prev
        for s in s_list:
            m_new = jnp.maximum(m_new, jnp.max(s, axis=-1, keepdims=True))
        alpha = jnp.exp(m_prev - m_new)
        m_sc[...] = m_new
        return m_new, alpha * l_sc[...], alpha * acc_sc[...]

    s_list = [jnp.dot(qbd, k_refs[j][0].astype(BF), preferred_element_type=F32) for j in range(DEC_PP)]
    m_new, l_new, acc = rescale(s_list)
    for j in range(DEC_PP):
        p = jnp.exp(s_list[j] - m_new)
        l_new = l_new + jnp.sum(p, axis=-1, keepdims=True)
        pe = jnp.dot(p.astype(BF), exp_ref[...], preferred_element_type=F32) * hm_ref[...]
        acc = acc + jnp.dot(pe.astype(BF), v_refs[j][0].astype(BF), preferred_element_type=F32)
    l_sc[...] = l_new
    acc_sc[...] = acc

    @pl.when(g == pl.num_programs(1) - 1)
    def _():
        kn = kn_ref[0].astype(BF)
        s = lax.dot_general(qbd, kn, (((1,), (1,)), ((), ())), preferred_element_type=F32)
        r = lax.broadcasted_iota(I32, s.shape, 0)
        c = lax.broadcasted_iota(I32, s.shape, 1)
        s = jnp.where(c <= (r % n_q), s, NEG_INF)
        m_fin, l_fin, acc_fin = rescale([s])
        p = jnp.exp(s - m_fin)
        l_fin = l_fin + jnp.sum(p, axis=-1, keepdims=True)
        pv = jnp.dot(p.astype(BF), vn_ref[0].astype(BF), preferred_element_type=F32)
        for h in range(ATT_HEADS):
            own = (rows // (2 * n_q)) == h
            acc_fin = acc_fin + jnp.where(own, pv[:, h * ATT_VD:(h + 1) * ATT_VD], 0.0)
        lam = _lambda_value(lam_ref, lam_init)
        o_all = acc_fin / l_fin
        for h in range(ATT_HEADS):
            blk = o_all[h * 2 * n_q:(h + 1) * 2 * n_q]
            o = blk[:n_q] - lam * blk[n_q:]
            o_ref[0, :, h * ATT_VD:(h + 1) * ATT_VD] = _subln(o, sg_ref[...], lam_init).astype(o_ref.dtype)


def _sample_attention(page_table, lam_p, qbd, kn, vn, sg, cache_kt, cache_vr, n_q, lam_init):
    nb, n_pages = page_table.shape
    width = cache_kt.shape[1]
    page = cache_kt.shape[2]
    col = jnp.arange(page * ATT_HEADS)
    expand = (col[None, :] // ATT_HEADS == jnp.arange(page)[:, None]).astype(BF)
    hmask = (col[None, :] % ATT_HEADS == jnp.arange(DEC_COLS)[:, None] // (2 * n_q)).astype(F32)

    def page_spec(arr, j):
        return pl.BlockSpec((1,) + arr.shape[1:], lambda b, g, pt: (pt[b, g * DEC_PP + j], 0, 0))

    const = lambda a: pl.BlockSpec(a.shape, lambda b, g, pt: (0,) * a.ndim)
    kern = functools.partial(_dec_attn_kernel, n_q=n_q, lam_init=lam_init)
    grid_spec = pltpu.PrefetchScalarGridSpec(
        num_scalar_prefetch=1,
        grid=(nb, n_pages // DEC_PP),
        in_specs=[
            const(lam_p),
            pl.BlockSpec((1, DEC_COLS, width), lambda b, g, pt: (b, 0, 0)),
            pl.BlockSpec((1, 8, width), lambda b, g, pt: (b, 0, 0)),
            pl.BlockSpec((1, 8, width), lambda b, g, pt: (b, 0, 0)),
            const(sg), const(expand), const(hmask),
        ] + [page_spec(cache_kt, j) for j in range(DEC_PP)] + [page_spec(cache_vr, j) for j in range(DEC_PP)],
        out_specs=pl.BlockSpec((1, n_q, width), lambda b, g, pt: (b, 0, 0)),
        scratch_shapes=[pltpu.VMEM((DEC_COLS, 1), F32), pltpu.VMEM((DEC_COLS, 1), F32),
                        pltpu.VMEM((DEC_COLS, ATT_VD), F32)],
    )
    return pl.pallas_call(
        kern,
        grid_spec=grid_spec,
        out_shape=jax.ShapeDtypeStruct((nb, n_q, width), BF),
        compiler_params=_cparams(("arbitrary", "arbitrary")),
        name="sample_attn",
    )(page_table, lam_p, qbd, kn, vn, sg, expand, hmask, *([cache_kt] * DEC_PP), *([cache_vr] * DEC_PP))


def _mix_kernel(x_ref, a_ref, u_ref, vg_ref, gate_ref, g1_ref, sh2_ref, sc2_ref, n2_ref,
                ws_ref, wmask_ref, bias_ref, wa_ref, wb_ref, wo_ref, x1_ref, h2_ref):
    tm = x_ref.shape[0]
    gc = GM_WIDTH // GM_GROUPS
    wmask = wmask_ref[...]
    bias = bias_ref[...]
    s_chunks = []
    for c in range(tm // GM_CHUNK):
        rows = slice(c * GM_CHUNK, (c + 1) * GM_CHUNK)
        cols = []
        for g in range(GM_GROUPS):
            w = (ws_ref[g] * wmask).astype(BF)
            vv = vg_ref[rows, g * gc:(g + 1) * gc].astype(BF)
            cols.append(jnp.dot(w, vv, preferred_element_type=F32))
        mixed = jnp.concatenate(cols, axis=1) + bias
        s_chunks.append((u_ref[rows, :].astype(F32) * mixed).astype(BF))
    s_out = jnp.concatenate(s_chunks, axis=0) if len(s_chunks) > 1 else s_chunks[0]
    ya = jnp.dot(a_ref[...], wa_ref[...], preferred_element_type=F32)
    yb = jnp.dot(s_out, wb_ref[...], preferred_element_type=F32)
    ga = gate_ref[:, :D_MODEL].astype(F32)
    gb = gate_ref[:, D_MODEL:].astype(F32)
    mix = (ga * ya + gb * yb).astype(BF)
    y = jnp.dot(mix, wo_ref[...], preferred_element_type=F32)
    x1 = x_ref[...] + g1_ref[0] * y
    x1_ref[...] = x1
    ms = jnp.mean(x1 * x1, axis=-1, keepdims=True)
    h2 = x1 * lax.rsqrt(ms + EPS) * n2_ref[...]
    h2 = h2 * (1.0 + sc2_ref[0]) + sh2_ref[0]
    h2_ref[...] = h2.astype(BF)


def _mix(x2, a_out, u, vg, gates, g1, sh2, sc2, n2, ws_exp, wmask, bias_full, wa, wb, wo, tm, rows_per_mod):
    r = x2.shape[0]
    rg = g1.shape[1]
    mod_spec = pl.BlockSpec((1, rg, D_MODEL), lambda i: ((i * tm) // rows_per_mod, 0, 0))
    row = lambda w: pl.BlockSpec((tm, w), lambda i: (i, 0))
    full = lambda a: pl.BlockSpec(a.shape, lambda i: (0,) * a.ndim)
    return pl.pallas_call(
        _mix_kernel,
        grid=(r // tm,),
        in_specs=[row(D_MODEL), row(D_MODEL), row(D_MODEL), row(D_MODEL), row(2 * D_MODEL),
                  mod_spec, mod_spec, mod_spec, full(n2), full(ws_exp), full(wmask), full(bias_full),
                  full(wa), full(wb), full(wo)],
        out_specs=[row(D_MODEL), row(D_MODEL)],
        out_shape=[jax.ShapeDtypeStruct((r, D_MODEL), F32), jax.ShapeDtypeStruct((r, D_MODEL), BF)],
        compiler_params=_cparams(("arbitrary",)),
        name="mix",
    )(x2, a_out, u, vg, gates, g1, sh2, sc2, n2, ws_exp, wmask, bias_full, wa, wb, wo)


def _top_rows(s, iota, n_rows, count):
    vals, idxs = [], []
    for _ in range(count):
        m = jnp.max(s, axis=0, keepdims=True)
        idx = jnp.min(jnp.where(s == m, iota, n_rows), axis=0, keepdims=True)
        vals.append(m)
        idxs.append(idx)
        s = jnp.where(iota == idx, NEG_INF, s)
    return vals, idxs


def _pair_candidates(v1, v2, j1, j2, sub8):
    nk = PEER_NKEYS
    v2_16 = jnp.concatenate(v2, axis=0)
    j2_16 = jnp.concatenate(j2, axis=0)
    v2_8, j2_8 = v2_16[:8], j2_16[:8]
    cand = [v1[0] + v2_16]
    cidx = [j1[0] * nk + j2_16]
    for a in range(1, 8):
        ok = sub8 < (PEER_TOPK // (a + 1))
        cand.append(jnp.where(ok, v1[a] + v2_8, NEG_INF))
        cidx.append(j1[a] * nk + j2_8)
    cand.append(jnp.concatenate(v1[8:], axis=0) + v2[0])
    cidx.append(jnp.concatenate(j1[8:], axis=0) * nk + j2[0])
    return jnp.concatenate(cand, axis=0), jnp.concatenate(cidx, axis=0)


PEER_NCAND = 80
PEER_HEADS_PER_ITER = 2


def _peer_topk_kernel(h2_ref, wq_ref, sk_ref, i1_ref, i2_ref, g_ref, qT_sc):
    tm = h2_ref.shape[0]
    nk = PEER_NKEYS
    qT_sc[...] = lax.dot_general(wq_ref[...], h2_ref[...], (((1,), (1,)), ((), ())),
                                 preferred_element_type=F32).astype(BF)
    iota_k = lax.broadcasted_iota(I32, (nk, tm), 0)
    iota_c = lax.broadcasted_iota(I32, (PEER_NCAND, tm), 0)
    sub8 = lax.broadcasted_iota(I32, (8, tm), 0)

    def body(it, carry):
        halves = []
        for dh in range(PEER_HEADS_PER_ITER):
            for p in range(2):
                hp = (it * PEER_HEADS_PER_ITER + dh) * 2 + p
                start = pl.multiple_of(hp * nk, nk)
                s = jnp.dot(sk_ref[hp], qT_sc[pl.ds(start, nk), :], preferred_element_type=F32)
                halves.append(_top_rows(s, iota_k, nk, PEER_TOPK))
        for dh in range(PEER_HEADS_PER_ITER):
            (v1, j1), (v2, j2) = halves[2 * dh], halves[2 * dh + 1]
            cand, cidx = _pair_candidates(v1, v2, j1, j2, sub8)
            tops, eids = [], []
            for _ in range(PEER_TOPK):
                m = jnp.max(cand, axis=0, keepdims=True)
                pos = jnp.min(jnp.where(cand == m, iota_c, PEER_NCAND), axis=0, keepdims=True)
                hit = iota_c == pos
                eids.append(jnp.max(jnp.where(hit, cidx, -1), axis=0, keepdims=True))
                tops.append(m)
                cand = jnp.where(hit, NEG_INF, cand)
            ts = jnp.concatenate(tops, axis=0)
            e = jnp.exp(ts - tops[0])
            gates = e / jnp.sum(e, axis=0, keepdims=True)
            eid = jnp.concatenate(eids, axis=0)
            row0 = pl.multiple_of((it * PEER_HEADS_PER_ITER + dh) * PEER_TOPK, PEER_TOPK)
            i1_ref[pl.ds(row0, PEER_TOPK), :] = lax.shift_right_logical(eid, 7).astype(F32)
            i2_ref[pl.ds(row0, PEER_TOPK), :] = lax.bitwise_and(eid, nk - 1).astype(F32)
            g_ref[pl.ds(row0, PEER_TOPK), :] = gates
        return carry

    lax.fori_loop(0, PEER_HEADS // PEER_HEADS_PER_ITER, body, 0)


def _peer_topk(h2, wqT, sk, tm):
    t = h2.shape[0]
    nq = wqT.shape[0]
    out = jax.ShapeDtypeStruct((PEER_SEL, t), F32)
    col = pl.BlockSpec((PEER_SEL, tm), lambda i: (0, i))
    return pl.pallas_call(
        _peer_topk_kernel,
        grid=(t // tm,),
        in_specs=[pl.BlockSpec((tm, D_MODEL), lambda i: (i, 0)),
                  pl.BlockSpec(wqT.shape, lambda i: (0, 0)),
                  pl.BlockSpec(sk.shape, lambda i: (0, 0, 0))],
        out_specs=[col, col, col],
        out_shape=[out, out, out],
        scratch_shapes=[pltpu.VMEM((nq, tm), BF)],
        compiler_params=_cparams(("arbitrary",)),
        name="peer_topk",
    )(h2, wqT, sk)


GATE_TG = 32


def _peer_gates_kernel(i1_ref, i2_ref, g_ref, perm_ref, o_ref, scr):
    tm = i1_ref.shape[0]
    nk = PEER_NKEYS
    ng = nk // 8
    iota = lax.broadcasted_iota(I32, (nk, PEER_SEL), 0).astype(F32)

    def body(t, carry):
        r1 = i1_ref[pl.ds(t, 1), :]
        r2 = i2_ref[pl.ds(t, 1), :]
        gg = g_ref[pl.ds(t, 1), :]
        a = jnp.where(iota == r1, gg, 0.0).astype(BF)
        b = jnp.where(iota == r2, 1.0, 0.0).astype(BF)
        gt = lax.dot_general(a, b, (((1,), (1,)), ((), ())), preferred_element_type=F32)
        row0 = pl.multiple_of(t * 8, 8)
        for q in range(ng):
            scr[q, pl.ds(row0, 8), :] = gt[q * 8:(q + 1) * 8, :]
        return carry

    lax.fori_loop(0, tm, body, 0, unroll=8)
    perm = perm_ref[...]
    rows = GATE_TG * 8
    for q in range(ng):
        for c in range(tm // GATE_TG):
            x = scr[q, c * rows:(c + 1) * rows, :].astype(BF)
            y = jnp.dot(perm, x, preferred_element_type=F32).astype(o_ref.dtype)
            for il in range(8):
                o_ref[q * 8 + il, c * GATE_TG:(c + 1) * GATE_TG, :] = y[il * GATE_TG:(il + 1) * GATE_TG, :]


def _peer_gates(i1, i2, g, perm, tm):
    t = i1.shape[0]
    row = pl.BlockSpec((tm, PEER_SEL), lambda i: (i, 0))
    return pl.pallas_call(
        _peer_gates_kernel,
        grid=(t // tm,),
        in_specs=[row, row, row, pl.BlockSpec(perm.shape, lambda i: (0, 0))],
        out_specs=pl.BlockSpec((PEER_NKEYS, tm, PEER_NKEYS), lambda i: (0, i, 0)),
        out_shape=jax.ShapeDtypeStruct((PEER_NKEYS, t, PEER_NKEYS), BF),
        scratch_shapes=[pltpu.VMEM((PEER_NKEYS // 8, tm * 8, PEER_NKEYS), F32)],
        compiler_params=_cparams(("arbitrary",)),
        name="peer_gates",
    )(i1, i2, g, perm)


PEER_EB = 8


def _peer_dense_kernel(h2_ref, x1_ref, g2_ref, gm_ref, u_ref, v_ref, o_ref, acc_sc):
    e = pl.program_id(1)

    @pl.when(e == 0)
    def _():
        acc_sc[...] = jnp.zeros(acc_sc.shape, F32)

    h2 = h2_ref[...]
    nk = PEER_NKEYS
    ws = []
    for il in range(PEER_EB):
        a = lax.dot_general(h2, u_ref[il * nk:(il + 1) * nk, :], (((1,), (1,)), ((), ())),
                            preferred_element_type=F32)
        ws.append((gm_ref[il].astype(F32) * _gelu(a)).astype(BF))
    w = jnp.concatenate(ws, axis=1)
    acc_sc[...] += jnp.dot(w, v_ref[...], preferred_element_type=F32)

    @pl.when(e == pl.num_programs(1) - 1)
    def _():
        o_ref[...] = x1_ref[...] + g2_ref[0] * acc_sc[...]


def _peer_dense(h2, x1, g2, gmat, u_b, v_b, tm, rows_per_mod):
    t = h2.shape[0]
    ne = u_b.shape[0]
    eb = PEER_EB * PEER_NKEYS
    rg = g2.shape[1]
    return pl.pallas_call(
        _peer_dense_kernel,
        grid=(t // tm, ne // eb),
        in_specs=[pl.BlockSpec((tm, D_MODEL), lambda i, e: (i, 0)),
                  pl.BlockSpec((tm, D_MODEL), lambda i, e: (i, 0)),
                  pl.BlockSpec((1, rg, D_MODEL), lambda i, e: ((i * tm) // rows_per_mod, 0, 0)),
                  pl.BlockSpec((PEER_EB, tm, PEER_NKEYS), lambda i, e: (e, i, 0)),
                  pl.BlockSpec((eb, D_MODEL), lambda i, e: (e, 0)),
                  pl.BlockSpec((eb, D_MODEL), lambda i, e: (e, 0))],
        out_specs=pl.BlockSpec((tm, D_MODEL), lambda i, e: (i, 0)),
        out_shape=jax.ShapeDtypeStruct((t, D_MODEL), F32),
        scratch_shapes=[pltpu.VMEM((tm, D_MODEL), F32)],
        compiler_params=_cparams(("arbitrary", "arbitrary")),
        name="peer_dense",
    )(h2, x1, g2, gmat, u_b, v_b)


def _rope_tables(pos, width=256):
    inv = 1.0 / (ROPE_THETA ** (jnp.arange(0, ATT_DH, 2, dtype=F32) / ATT_DH))
    ang = pos.astype(F32)[:, None] * inv[None, :]
    cos = jnp.cos(ang)
    sin = jnp.sin(ang)
    reps = width // ATT_DH
    cos_t = jnp.tile(jnp.concatenate([cos, cos], axis=1), (1, reps))
    sin_t = jnp.tile(jnp.concatenate([-sin, sin], axis=1), (1, reps))
    return cos_t, sin_t


def _block_diag_ones(width, seg):
    r = jnp.arange(width) // seg
    return (r[:, None] == r[None, :]).astype(BF)


def _peer(h2, x1, g2, wqT, sk, u_b, v_b, tm_topk, tm_gates, tm_dense, rows_per_mod):
    i1t, i2t, gt = _peer_topk(h2, wqT, sk, tm_topk)
    r_out = jnp.arange(GATE_TG * 8)
    src = (r_out % GATE_TG) * 8 + r_out // GATE_TG
    perm = (jnp.arange(GATE_TG * 8)[None, :] == src[:, None]).astype(BF)
    gmat = _peer_gates(i1t.T, i2t.T, gt.T, perm, tm_gates)
    return _peer_dense(h2, x1, g2, gmat, u_b, v_b, tm_dense, rows_per_mod)


def kernel(x_prompt, x_sample, cache_k, cache_v, page_table, c_prompt, c_sample, w_ada, b_ada, norm1_g, w_in, q_norm_g, k_norm_g, lambda_q1, lambda_k1, lambda_q2, lambda_k2, subln_g, gm_v_norm_g, gm_ws, gm_bs, w_att_out, w_gm_out, w_out, norm2_g, peer_w_q, peer_subkeys, peer_u, peer_v):
    depth = w_ada.shape[0]
    assert depth == 1
    l = 0
    lam_init = 0.8 - 0.6 * math.exp(-0.3 * l)
    bsz, t_p, d = x_prompt.shape
    nb, t_s, _ = x_sample.shape
    n_pool, page = cache_k.shape[1], cache_k.shape[2]
    past_len = page_table.shape[1] * page

    w_in_b = w_in[l].astype(BF)
    wa_b = w_att_out[l].astype(BF)
    wb_b = w_gm_out[l].astype(BF)
    wo_b = w_out[l].astype(BF)
    wqT_b = peer_w_q[l].T.astype(BF)
    sk_b = peer_subkeys[l].reshape(2 * PEER_HEADS, PEER_NKEYS, -1).astype(BF)
    u_b = peer_u[l].astype(BF)
    v_b = peer_v[l].astype(BF)
    g1n = norm1_g[l].reshape(1, d)
    g2n = norm2_g[l].reshape(1, d)
    gq = jnp.tile(q_norm_g[l], 256 // ATT_DH).reshape(1, 256)
    gk = jnp.tile(k_norm_g[l], 256 // ATT_DH).reshape(1, 256)
    gv = jnp.tile(gm_v_norm_g[l], 2).reshape(1, 256)
    sg = subln_g[l].reshape(1, ATT_VD)
    lam_p = jnp.stack([lambda_q1[l], lambda_k1[l], lambda_q2[l], lambda_k2[l]], axis=0)
    seg64 = _block_diag_ones(256, ATT_DH)
    seg128 = _block_diag_ones(256, GM_WIDTH // GM_GROUPS)
    bias_p = jnp.repeat(gm_bs[l].T, GM_WIDTH // GM_GROUPS, axis=1)
    ri = jnp.arange(GM_CHUNK)
    tril_p = (ri[None, :] <= ri[:, None]).astype(F32)
    ws_s = jnp.tile(gm_ws[l][:, :t_s, :t_s], (1, GM_CHUNK // t_s, GM_CHUNK // t_s))
    mask_s = ((ri[:, None] // t_s == ri[None, :] // t_s) & (ri[None, :] % t_s <= ri[:, None] % t_s)).astype(F32)
    bias_s = jnp.tile(bias_p[:t_s], (GM_CHUNK // t_s, 1))

    n_c = bsz + nb
    c_all = jnp.concatenate([c_prompt, c_sample, jnp.zeros((-n_c % 8, d), F32)], axis=0)
    mod = _adaln(c_all, w_ada[l], b_ada[l])
    mod_p = mod[:bsz].reshape(bsz, 1, 6, d)
    mod_s = jnp.repeat(mod[bsz:n_c], t_s, axis=0).reshape(1, nb * t_s, 6, d)
    mp = [mod_p[:, :, i, :] for i in range(6)]
    ms_ = [mod_s[:, :, i, :] for i in range(6)]

    xp2 = x_prompt.reshape(bsz * t_p, d)
    cos_p, sin_p = _rope_tables(jnp.arange(t_p))
    cos_pt = jnp.tile(cos_p, (bsz, 1))
    sin_pt = jnp.tile(sin_p, (bsz, 1))
    q_p, k_p, kb_p, v_p, vb_p, u_p, vg_p, gates_p = _proj_in(
        xp2, mp[0], mp[1], g1n, w_in_b, cos_pt, sin_pt, gq, gk, gv, seg64, seg128,
        tm=512, rows_per_mod=t_p, vg_dtype=BF, k_transposed=True)
    a_p = _prompt_attention(lam_p, q_p.reshape(bsz, t_p, d), kb_p, vb_p.reshape(bsz, t_p, d), sg, lam_init)
    x1_p, h2_p = _mix(xp2, a_p.reshape(bsz * t_p, d), u_p, vg_p, gates_p, mp[2], mp[3], mp[4], g2n,
                      gm_ws[l], tril_p, bias_p, wa_b, wb_b, wo_b, tm=256, rows_per_mod=t_p)
    y_p = _peer(h2_p, x1_p, mp[5], wqT_b, sk_b, u_b, v_b,
                tm_topk=128, tm_gates=128, tm_dense=1024, rows_per_mod=t_p)

    r_s = nb * t_s
    xs2 = x_sample.reshape(r_s, d)
    cos_s, sin_s = _rope_tables(past_len + jnp.arange(t_s))
    cos_st = jnp.tile(cos_s, (nb, 1))
    sin_st = jnp.tile(sin_s, (nb, 1))
    q_s, k_s, _, v_s, _, u_s, vg_s, gates_s = _proj_in(
        xs2, ms_[0], ms_[1], g1n, w_in_b, cos_st, sin_st, gq, gk, gv, seg64, seg128,
        tm=r_s, rows_per_mod=r_s, vg_dtype=F32, k_transposed=False)
    q3 = q_s.reshape(nb, t_s, d)
    q_rep = jnp.tile(q3, (1, DEC_COLS // t_s, 1))
    rr = jnp.arange(DEC_COLS)[:, None] // t_s
    cc = jnp.arange(d)[None, :] // ATT_DH
    qbd = jnp.where(rr == cc, q_rep, jnp.zeros_like(q_rep))
    pad8 = lambda a: jnp.pad(a.reshape(nb, t_s, d), ((0, 0), (0, 8 - t_s), (0, 0)))
    cache_kt = cache_k[l].transpose(0, 2, 3, 4, 1).reshape(n_pool, d, page)
    cache_vr = cache_v[l].reshape(n_pool, page * ATT_HEADS, ATT_VD)
    a_s = _sample_attention(page_table, lam_p, qbd, pad8(k_s), pad8(v_s), sg, cache_kt, cache_vr,
                            n_q=t_s, lam_init=lam_init)
    x1_s, h2_s = _mix(xs2, a_s.reshape(r_s, d), u_s, vg_s, gates_s, ms_[2], ms_[3], ms_[4], g2n,
                      ws_s, mask_s, bias_s, wa_b, wb_b, wo_b, tm=r_s, rows_per_mod=r_s)
    y_s = _peer(h2_s, x1_s, ms_[5], wqT_b, sk_b, u_b, v_b,
                tm_topk=r_s, tm_gates=r_s, tm_dense=r_s, rows_per_mod=r_s)

    return (
        y_p.reshape(bsz, t_p, d),
        y_s.reshape(nb, t_s, d),
        k_p.reshape(1, bsz, ATT_HEADS, 2, ATT_DH, t_p).transpose(0, 1, 5, 2, 3, 4),
        v_p.reshape(1, bsz, t_p, ATT_HEADS, ATT_VD),
        k_s.reshape(1, nb, t_s, ATT_HEADS, 2, ATT_DH),
        v_s.reshape(1, nb, t_s, ATT_HEADS, ATT_VD),
        vg_s.reshape(1, nb, t_s, GM_WIDTH),
    )
```

```python
import functools
import math

import jax
import jax.numpy as jnp
from jax import lax
from jax.experimental import pallas as pl
from jax.experimental.pallas import tpu as pltpu

BF = jnp.bfloat16
F32 = jnp.float32
I32 = jnp.int32

D_MODEL = 1024
ATT_HEADS = 8
ATT_DH = 64
ATT_VD = 128
ROPE_THETA = 10000.0
GM_GROUPS = 8
GM_CHUNK = 128
GM_WIDTH = 1024
IN_WIDTH = 7168
PEER_HEADS = 8
PEER_NKEYS = 128
PEER_TOPK = 16
PEER_SEL = PEER_HEADS * PEER_TOPK
EPS = 1e-6
NEG_INF = float("-inf")

VMEM_LIMIT = 56 * 1024 * 1024


def _cparams(sem):
    return pltpu.CompilerParams(dimension_semantics=sem, vmem_limit_bytes=VMEM_LIMIT)


def _gelu(x):
    return 0.5 * x * (1.0 + lax.erf(x * math.sqrt(0.5)))


def _seg_mean_sq(z, seg, n):
    z2 = z * z
    hi = z2.astype(BF)
    lo = (z2 - hi.astype(F32)).astype(BF)
    ss = jnp.dot(hi, seg, preferred_element_type=F32) + jnp.dot(lo, seg, preferred_element_type=F32)
    return ss * (1.0 / n)


def _ada_kernel(c_ref, w_ref, b_ref, o_ref):
    c = c_ref[...]
    s = c * jax.nn.sigmoid(c)
    o_ref[...] = jnp.dot(s.astype(BF), w_ref[...].astype(BF), preferred_element_type=F32) + b_ref[...]


def _adaln(c_all, w_ada, b_ada):
    r = c_all.shape[0]
    n = w_ada.shape[1]
    tn = 1024
    return pl.pallas_call(
        _ada_kernel,
        grid=(n // tn,),
        in_specs=[
            pl.BlockSpec((r, D_MODEL), lambda j: (0, 0)),
            pl.BlockSpec((D_MODEL, tn), lambda j: (0, j)),
            pl.BlockSpec((1, tn), lambda j: (0, j)),
        ],
        out_specs=pl.BlockSpec((r, tn), lambda j: (0, j)),
        out_shape=jax.ShapeDtypeStruct((r, n), F32),
        compiler_params=_cparams(("arbitrary",)),
        name="adaln",
    )(c_all, w_ada, b_ada.reshape(1, n))


def _proj_in_kernel(x_ref, sh_ref, sc_ref, g1_ref, w_ref, cos_ref, sin_ref, gq_ref, gk_ref, gv_ref,
                    seg64_ref, seg128_ref,
                    q_ref, k_ref, kb_ref, v_ref, vb_ref, u_ref, vg_ref, gate_ref, *, k_transposed):
    x = x_ref[...]
    ms = jnp.mean(x * x, axis=-1, keepdims=True)
    h = x * lax.rsqrt(ms + EPS) * g1_ref[...]
    h = h * (1.0 + sc_ref[0]) + sh_ref[0]
    hb = h.astype(BF)
    cos = cos_ref[...]
    sin = sin_ref[...]
    seg64 = seg64_ref[...]
    seg128 = seg128_ref[...]
    tm = x.shape[0]
    cw = 256
    lane = lax.broadcasted_iota(I32, (tm, cw), 1)
    first_half = (lane % ATT_DH) < (ATT_DH // 2)

    def qk_tile(col, g):
        z = jnp.dot(hb, w_ref[:, col:col + cw], preferred_element_type=F32)
        y = z * lax.rsqrt(_seg_mean_sq(z, seg64, ATT_DH) + EPS) * g
        rot = jnp.where(first_half, pltpu.roll(y, cw - ATT_DH // 2, 1), pltpu.roll(y, ATT_DH // 2, 1))
        return y * cos + rot * sin

    for t in range(D_MODEL // cw):
        c = t * cw
        q = qk_tile(c, gq_ref[...])
        q_ref[:, c:c + cw] = (q * (ATT_DH ** -0.5)).astype(BF)
        k = qk_tile(D_MODEL + c, gk_ref[...])
        if k_transposed:
            kt = k.T
            k_ref[c:c + cw, :] = kt
            kb_ref[c:c + cw, :] = kt.astype(BF)
        else:
            k_ref[:, c:c + cw] = k
            kb_ref[:, c:c + cw] = k.astype(BF)
        v = jnp.dot(hb, w_ref[:, 2 * D_MODEL + c:2 * D_MODEL + c + cw], preferred_element_type=F32)
        v_ref[:, c:c + cw] = v
        vb_ref[:, c:c + cw] = v.astype(BF)
        zu = jnp.dot(hb, w_ref[:, 3 * D_MODEL + c:3 * D_MODEL + c + cw], preferred_element_type=F32)
        u_ref[:, c:c + cw] = _gelu(zu).astype(BF)
        zv = jnp.dot(hb, w_ref[:, 4 * D_MODEL + c:4 * D_MODEL + c + cw], preferred_element_type=F32)
        gv = _gelu(zv)
        vg = gv * lax.rsqrt(_seg_mean_sq(gv, seg128, GM_WIDTH // GM_GROUPS) + EPS) * gv_ref[...]
        vg_ref[:, c:c + cw] = vg.astype(vg_ref.dtype)
    for t in range(2 * D_MODEL // cw):
        c = t * cw
        zg = jnp.dot(hb, w_ref[:, 5 * D_MODEL + c:5 * D_MODEL + c + cw], preferred_element_type=F32)
        gate_ref[:, c:c + cw] = jax.nn.sigmoid(zg).astype(BF)


def _proj_in(x2, sh, sc, g1, w_in_b, cos, sin, gq, gk, gv, seg64, seg128, tm, rows_per_mod, vg_dtype,
             k_transposed):
    r = x2.shape[0]
    rg = sh.shape[1]
    mod_spec = pl.BlockSpec((1, rg, D_MODEL), lambda i: ((i * tm) // rows_per_mod, 0, 0))
    row = lambda w: pl.BlockSpec((tm, w), lambda i: (i, 0))
    full = lambda a: pl.BlockSpec(a.shape, lambda i: (0,) * a.ndim, pipeline_mode=pl.Buffered(1))
    if k_transposed:
        tiles = rows_per_mod // tm
        k_shape = (r // rows_per_mod, D_MODEL, rows_per_mod)
        k_spec = pl.BlockSpec((None, D_MODEL, tm), lambda i: (i // tiles, 0, i % tiles))
    else:
        k_shape = (r, D_MODEL)
        k_spec = row(D_MODEL)
    out_shapes = [
        jax.ShapeDtypeStruct((r, D_MODEL), BF),
        jax.ShapeDtypeStruct(k_shape, F32),
        jax.ShapeDtypeStruct(k_shape, BF),
        jax.ShapeDtypeStruct((r, D_MODEL), F32),
        jax.ShapeDtypeStruct((r, D_MODEL), BF),
        jax.ShapeDtypeStruct((r, D_MODEL), BF),
        jax.ShapeDtypeStruct((r, D_MODEL), vg_dtype),
        jax.ShapeDtypeStruct((r, 2 * D_MODEL), BF),
    ]
    return pl.pallas_call(
        functools.partial(_proj_in_kernel, k_transposed=k_transposed),
        grid=(r // tm,),
        in_specs=[row(D_MODEL), mod_spec, mod_spec, full(g1), full(w_in_b), row(256), row(256),
                  full(gq), full(gk), full(gv), full(seg64), full(seg128)],
        out_specs=[row(D_MODEL), k_spec, k_spec] + [row(D_MODEL)] * 4 + [row(2 * D_MODEL)],
        out_shape=out_shapes,
        compiler_params=_cparams(("arbitrary",)),
        name="proj_in",
    )(x2, sh, sc, g1, w_in_b, cos, sin, gq, gk, gv, seg64, seg128)


def _lambda_value(lam_ref, lam_init):
    lp = lam_ref[...]
    a = jnp.sum(lp[0:1] * lp[1:2], axis=-1, keepdims=True)
    b = jnp.sum(lp[2:3] * lp[3:4], axis=-1, keepdims=True)
    return jnp.exp(a) - jnp.exp(b) + lam_init


def _subln(o, sg, lam_init):
    ss = jnp.mean(o * o, axis=-1, keepdims=True)
    return o * lax.rsqrt(ss + EPS) * sg * (1.0 - lam_init)


def _attn_kernel(lam_ref, q_ref, k_ref, v_ref, sg_ref, o_ref, vx_sc, m_sc, acc_sc, *, tq, lam_init):
    qi = pl.program_id(2)
    t = v_ref.shape[0]

    @pl.when(qi == 0)
    def _():
        vx_sc[:, :ATT_VD] = v_ref[...]
        vx_sc[:, ATT_VD:] = jnp.ones((t, ATT_VD), BF)

    q = q_ref[...]
    lane = lax.broadcasted_iota(I32, q.shape, 1)
    zero = jnp.zeros_like(q)
    q2 = jnp.concatenate([jnp.where(lane < ATT_DH, q, zero), jnp.where(lane >= ATT_DH, q, zero)], axis=0)
    m_sc[...] = jnp.full(m_sc.shape, NEG_INF, F32)
    acc_sc[...] = jnp.zeros(acc_sc.shape, F32)

    def step(j, masked):
        start = pl.multiple_of(j * tq, tq)
        s = jnp.dot(q2, k_ref[:, pl.ds(start, tq)], preferred_element_type=F32)
        if masked:
            r = lax.broadcasted_iota(I32, s.shape, 0)
            c = lax.broadcasted_iota(I32, s.shape, 1)
            qpos = jnp.where(r >= tq, r - tq, r)
            s = jnp.where(c <= qpos, s, NEG_INF)
        m_prev = m_sc[...]
        m_new = jnp.maximum(m_prev, jnp.max(s, axis=-1, keepdims=True))
        alpha = jnp.exp(m_prev - m_new)
        p = jnp.exp(s - jnp.tile(m_new, (1, tq // ATT_VD)))
        pv = jnp.dot(p.astype(BF), vx_sc[pl.ds(start, tq), :], preferred_element_type=F32)
        acc_sc[...] = jnp.tile(alpha, (1, 2)) * acc_sc[...] + pv
        m_sc[...] = m_new

    def body(j, carry):
        step(j, False)
        return carry

    lax.fori_loop(0, qi, body, 0)
    step(qi, True)

    lam = _lambda_value(lam_ref, lam_init)
    o_all = acc_sc[:, :ATT_VD] / acc_sc[:, ATT_VD:]
    o = o_all[:tq] - lam * o_all[tq:]
    o_ref[...] = _subln(o, sg_ref[...], lam_init).astype(o_ref.dtype)


def _prompt_attention(lam_p, qb, kbt, vb, sg, lam_init, tq=512):
    b, t, _ = qb.shape
    kern = functools.partial(_attn_kernel, tq=tq, lam_init=lam_init)
    return pl.pallas_call(
        kern,
        grid=(b, ATT_HEADS, t // tq),
        in_specs=[
            pl.BlockSpec(lam_p.shape, lambda bi, h, i: (0, 0)),
            pl.BlockSpec((None, tq, ATT_VD), lambda bi, h, i: (bi, i, h)),
            pl.BlockSpec((None, ATT_VD, t), lambda bi, h, i: (bi, h, 0)),
            pl.BlockSpec((None, t, ATT_VD), lambda bi, h, i: (bi, 0, h)),
            pl.BlockSpec(sg.shape, lambda bi, h, i: (0, 0)),
        ],
        out_specs=pl.BlockSpec((None, tq, ATT_VD), lambda bi, h, i: (bi, i, h)),
        out_shape=jax.ShapeDtypeStruct((b, t, ATT_HEADS * ATT_VD), BF),
        scratch_shapes=[pltpu.VMEM((t, 2 * ATT_VD), BF), pltpu.VMEM((2 * tq, ATT_VD), F32),
                        pltpu.VMEM((2 * tq, 2 * ATT_VD), F32)],
        compiler_params=_cparams(("arbitrary", "arbitrary", "arbitrary")),
        name="prompt_attn",
    )(lam_p, qb, kbt, vb, sg)


DEC_PP = 8
DEC_COLS = 64


def _dec_attn_kernel(pt_ref, lam_ref, qbd_ref, kn_ref, vn_ref, sg_ref, exp_ref, hm_ref, *rest, n_q, lam_init):
    k_refs = rest[:DEC_PP]
    v_refs = rest[DEC_PP:2 * DEC_PP]
    o_ref = rest[2 * DEC_PP]
    m_sc, l_sc, acc_sc = rest[2 * DEC_PP + 1:]
    g = pl.program_id(1)
    qbd = qbd_ref[0]
    rows = lax.broadcasted_iota(I32, (DEC_COLS, ATT_VD), 0)

    @pl.when(g == 0)
    def _():
        m_sc[...] = jnp.full(m_sc.shape, NEG_INF, F32)
        l_sc[...] = jnp.zeros(l_sc.shape, F32)
        acc_sc[...] = jnp.zeros(acc_sc.shape, F32)

    def rescale(s_list):
        m_prev = m_sc[...]
        m_new = m_prev
        for s in s_list:
            m_new = jnp.maximum(m_new, jnp.max(s, axis=-1, keepdims=True))
        alpha = jnp.exp(m_prev - m_new)
        m_sc[...] = m_new
        return m_new, alpha * l_sc[...], alpha * acc_sc[...]

    s_list = [jnp.dot(qbd, k_refs[j][0].astype(BF), preferred_element_type=F32) for j in range(DEC_PP)]
    m_new, l_new, acc = rescale(s_list)
    for j in range(DEC_PP):
        p = jnp.exp(s_list[j] - m_new)
        l_new = l_new + jnp.sum(p, axis=-1, keepdims=True)
        pe = jnp.dot(p.astype(BF), exp_ref[...], preferred_element_type=F32) * hm_ref[...]
        acc = acc + jnp.dot(pe.astype(BF), v_refs[j][0].astype(BF), preferred_element_type=F32)
    l_sc[...] = l_new
    acc_sc[...] = acc

    @pl.when(g == pl.num_programs(1) - 1)
    def _():
        kn = kn_ref[0].astype(BF)
        s = lax.dot_general(qbd, kn, (((1,), (1,)), ((), ())), preferred_element_type=F32)
        r = lax.broadcasted_iota(I32, s.shape, 0)
        c = lax.broadcasted_iota(I32, s.shape, 1)
        s = jnp.where(c <= (r % n_q), s, NEG_INF)
        m_fin, l_fin, acc_fin = rescale([s])
        p = jnp.exp(s - m_fin)
        l_fin = l_fin + jnp.sum(p, axis=-1, keepdims=True)
        pv = jnp.dot(p.astype(BF), vn_ref[0].astype(BF), preferred_element_type=F32)
        for h in range(ATT_HEADS):
            own = (rows // (2 * n_q)) == h
            acc_fin = acc_fin + jnp.where(own, pv[:, h * ATT_VD:(h + 1) * ATT_VD], 0.0)
        lam = _lambda_value(lam_ref, lam_init)
        o_all = acc_fin / l_fin
        for h in range(ATT_HEADS):
            blk = o_all[h * 2 * n_q:(h + 1) * 2 * n_q]
            o = blk[:n_q] - lam * blk[n_q:]
            o_ref[0, :, h * ATT_VD:(h + 1) * ATT_VD] = _subln(o, sg_ref[...], lam_init).astype(o_ref.dtype)


def _sample_attention(page_table, lam_p, qbd, kn, vn, sg, cache_kt, cache_vr, n_q, lam_init):
    nb, n_pages = page_table.shape
    width = cache_kt.shape[1]
    page = cache_kt.shape[2]
    col = jnp.arange(page * ATT_HEADS)
    expand = (col[None, :] // ATT_HEADS == jnp.arange(page)[:, None]).astype(BF)
    hmask = (col[None, :] % ATT_HEADS == jnp.arange(DEC_COLS)[:, None] // (2 * n_q)).astype(F32)

    def page_spec(arr, j):
        return pl.BlockSpec((1,) + arr.shape[1:], lambda b, g, pt: (pt[b, g * DEC_PP + j], 0, 0))

    const = lambda a: pl.BlockSpec(a.shape, lambda b, g, pt: (0,) * a.ndim)
    kern = functools.partial(_dec_attn_kernel, n_q=n_q, lam_init=lam_init)
    grid_spec = pltpu.PrefetchScalarGridSpec(
        num_scalar_prefetch=1,
        grid=(nb, n_pages // DEC_PP),
        in_specs=[
            const(lam_p),
            pl.BlockSpec((1, DEC_COLS, width), lambda b, g, pt: (b, 0, 0)),
            pl.BlockSpec((1, 8, width), lambda b, g, pt: (b, 0, 0)),
            pl.BlockSpec((1, 8, width), lambda b, g, pt: (b, 0, 0)),
            const(sg), const(expand), const(hmask),
        ] + [page_spec(cache_kt, j) for j in range(DEC_PP)] + [page_spec(cache_vr, j) for j in range(DEC_PP)],
        out_specs=pl.BlockSpec((1, n_q, width), lambda b, g, pt: (b, 0, 0)),
        scratch_shapes=[pltpu.VMEM((DEC_COLS, 1), F32), pltpu.VMEM((DEC_COLS, 1), F32),
                        pltpu.VMEM((DEC_COLS, ATT_VD), F32)],
    )
    return pl.pallas_call(
        kern,
        grid_spec=grid_spec,
        out_shape=jax.ShapeDtypeStruct((nb, n_q, width), BF),
        compiler_params=_cparams(("arbitrary", "arbitrary")),
        name="sample_attn",
    )(page_table, lam_p, qbd, kn, vn, sg, expand, hmask, *([cache_kt] * DEC_PP), *([cache_vr] * DEC_PP))


def _mix_kernel(x_ref, a_ref, u_ref, vg_ref, gate_ref, g1_ref, sh2_ref, sc2_ref, n2_ref,
                ws_ref, wmask_ref, bias_ref, wa_ref, wb_ref, wo_ref, x1_ref, h2_ref):
    tm = x_ref.shape[0]
    gc = GM_WIDTH // GM_GROUPS
    wmask = wmask_ref[...]
    bias = bias_ref[...]
    s_chunks = []
    for c in range(tm // GM_CHUNK):
        rows = slice(c * GM_CHUNK, (c + 1) * GM_CHUNK)
        cols = []
        for g in range(GM_GROUPS):
            w = (ws_ref[g] * wmask).astype(BF)
            vv = vg_ref[rows, g * gc:(g + 1) * gc].astype(BF)
            cols.append(jnp.dot(w, vv, preferred_element_type=F32))
        mixed = jnp.concatenate(cols, axis=1) + bias
        s_chunks.append((u_ref[rows, :].astype(F32) * mixed).astype(BF))
    s_out = jnp.concatenate(s_chunks, axis=0) if len(s_chunks) > 1 else s_chunks[0]
    ya = jnp.dot(a_ref[...], wa_ref[...], preferred_element_type=F32)
    yb = jnp.dot(s_out, wb_ref[...], preferred_element_type=F32)
    ga = gate_ref[:, :D_MODEL].astype(F32)
    gb = gate_ref[:, D_MODEL:].astype(F32)
    mix = (ga * ya + gb * yb).astype(BF)
    y = jnp.dot(mix, wo_ref[...], preferred_element_type=F32)
    x1 = x_ref[...] + g1_ref[0] * y
    x1_ref[...] = x1
    ms = jnp.mean(x1 * x1, axis=-1, keepdims=True)
    h2 = x1 * lax.rsqrt(ms + EPS) * n2_ref[...]
    h2 = h2 * (1.0 + sc2_ref[0]) + sh2_ref[0]
    h2_ref[...] = h2.astype(BF)


def _mix(x2, a_out, u, vg, gates, g1, sh2, sc2, n2, ws_exp, wmask, bias_full, wa, wb, wo, tm, rows_per_mod):
    r = x2.shape[0]
    rg = g1.shape[1]
    mod_spec = pl.BlockSpec((1, rg, D_MODEL), lambda i: ((i * tm) // rows_per_mod, 0, 0))
    row = lambda w: pl.BlockSpec((tm, w), lambda i: (i, 0))
    full = lambda a: pl.BlockSpec(a.shape, lambda i: (0,) * a.ndim)
    return pl.pallas_call(
        _mix_kernel,
        grid=(r // tm,),
        in_specs=[row(D_MODEL), row(D_MODEL), row(D_MODEL), row(D_MODEL), row(2 * D_MODEL),
                  mod_spec, mod_spec, mod_spec, full(n2), full(ws_exp), full(wmask), full(bias_full),
                  full(wa), full(wb), full(wo)],
        out_specs=[row(D_MODEL), row(D_MODEL)],
        out_shape=[jax.ShapeDtypeStruct((r, D_MODEL), F32), jax.ShapeDtypeStruct((r, D_MODEL), BF)],
        compiler_params=_cparams(("arbitrary",)),
        name="mix",
    )(x2, a_out, u, vg, gates, g1, sh2, sc2, n2, ws_exp, wmask, bias_full, wa, wb, wo)


def _top_rows(s, iota, n_rows, count):
    vals, idxs = [], []
    for _ in range(count):
        m = jnp.max(s, axis=0, keepdims=True)
        idx = jnp.min(jnp.where(s == m, iota, n_rows), axis=0, keepdims=True)
        vals.append(m)
        idxs.append(idx)
        s = jnp.where(iota == idx, NEG_INF, s)
    return vals, idxs


def _pair_candidates(v1, v2, j1, j2, sub8):
    nk = PEER_NKEYS
    v2_16 = jnp.concatenate(v2, axis=0)
    j2_16 = jnp.concatenate(j2, axis=0)
    v2_8, j2_8 = v2_16[:8], j2_16[:8]
    cand = [v1[0] + v2_16]
    cidx = [j1[0] * nk + j2_16]
    for a in range(1, 8):
        ok = sub8 < (PEER_TOPK // (a + 1))
        cand.append(jnp.where(ok, v1[a] + v2_8, NEG_INF))
        cidx.append(j1[a] * nk + j2_8)
    cand.append(jnp.concatenate(v1[8:], axis=0) + v2[0])
    cidx.append(jnp.concatenate(j1[8:], axis=0) * nk + j2[0])
    return jnp.concatenate(cand, axis=0), jnp.concatenate(cidx, axis=0)


PEER_NCAND = 80
PEER_HEADS_PER_ITER = 2


def _peer_topk_kernel(h2_ref, wq_ref, sk_ref, i1_ref, i2_ref, g_ref, qT_sc):
    tm = h2_ref.shape[0]
    nk = PEER_NKEYS
    qT_sc[...] = lax.dot_general(wq_ref[...], h2_ref[...], (((1,), (1,)), ((), ())),
                                 preferred_element_type=F32).astype(BF)
    iota_k = lax.broadcasted_iota(I32, (nk, tm), 0)
    iota_c = lax.broadcasted_iota(I32, (PEER_NCAND, tm), 0)
    sub8 = lax.broadcasted_iota(I32, (8, tm), 0)

    def body(it, carry):
        halves = []
        for dh in range(PEER_HEADS_PER_ITER):
            for p in range(2):
                hp = (it * PEER_HEADS_PER_ITER + dh) * 2 + p
                start = pl.multiple_of(hp * nk, nk)
                s = jnp.dot(sk_ref[hp], qT_sc[pl.ds(start, nk), :], preferred_element_type=F32)
                halves.append(_top_rows(s, iota_k, nk, PEER_TOPK))
        for dh in range(PEER_HEADS_PER_ITER):
            (v1, j1), (v2, j2) = halves[2 * dh], halves[2 * dh + 1]
            cand, cidx = _pair_candidates(v1, v2, j1, j2, sub8)
            tops, eids = [], []
            for _ in range(PEER_TOPK):
                m = jnp.max(cand, axis=0, keepdims=True)
                pos = jnp.min(jnp.where(cand == m, iota_c, PEER_NCAND), axis=0, keepdims=True)
                hit = iota_c == pos
                eids.append(jnp.max(jnp.where(hit, cidx, -1), axis=0, keepdims=True))
                tops.append(m)
                cand = jnp.where(hit, NEG_INF, cand)
            ts = jnp.concatenate(tops, axis=0)
            e = jnp.exp(ts - tops[0])
            gates = e / jnp.sum(e, axis=0, keepdims=True)
            eid = jnp.concatenate(eids, axis=0)
            row0 = pl.multiple_of((it * PEER_HEADS_PER_ITER + dh) * PEER_TOPK, PEER_TOPK)
            i1_ref[pl.ds(row0, PEER_TOPK), :] = lax.shift_right_logical(eid, 7).astype(F32)
            i2_ref[pl.ds(row0, PEER_TOPK), :] = lax.bitwise_and(eid, nk - 1).astype(F32)
            g_ref[pl.ds(row0, PEER_TOPK), :] = gates
        return carry

    lax.fori_loop(0, PEER_HEADS // PEER_HEADS_PER_ITER, body, 0)


def _peer_topk(h2, wqT, sk, tm):
    t = h2.shape[0]
    nq = wqT.shape[0]
    out = jax.ShapeDtypeStruct((PEER_SEL, t), F32)
    col = pl.BlockSpec((PEER_SEL, tm), lambda i: (0, i))
    return pl.pallas_call(
        _peer_topk_kernel,
        grid=(t // tm,),
        in_specs=[pl.BlockSpec((tm, D_MODEL), lambda i: (i, 0)),
                  pl.BlockSpec(wqT.shape, lambda i: (0, 0)),
                  pl.BlockSpec(sk.shape, lambda i: (0, 0, 0))],
        out_specs=[col, col, col],
        out_shape=[out, out, out],
        scratch_shapes=[pltpu.VMEM((nq, tm), BF)],
        compiler_params=_cparams(("arbitrary",)),
        name="peer_topk",
    )(h2, wqT, sk)


GATE_TG = 32


def _peer_gates_kernel(i1_ref, i2_ref, g_ref, perm_ref, o_ref, scr):
    tm = i1_ref.shape[0]
    nk = PEER_NKEYS
    ng = nk // 8
    iota = lax.broadcasted_iota(I32, (nk, PEER_SEL), 0).astype(F32)

    def body(t, carry):
        r1 = i1_ref[pl.ds(t, 1), :]
        r2 = i2_ref[pl.ds(t, 1), :]
        gg = g_ref[pl.ds(t, 1), :]
        a = jnp.where(iota == r1, gg, 0.0).astype(BF)
        b = jnp.where(iota == r2, 1.0, 0.0).astype(BF)
        gt = lax.dot_general(a, b, (((1,), (1,)), ((), ())), preferred_element_type=F32)
        row0 = pl.multiple_of(t * 8, 8)
        for q in range(ng):
            scr[q, pl.ds(row0, 8), :] = gt[q * 8:(q + 1) * 8, :]
        return carry

    lax.fori_loop(0, tm, body, 0, unroll=GATE_TG)
    perm = perm_ref[...]
    rows = GATE_TG * 8
    for q in range(0, ng, 2):
        for c in range(tm // GATE_TG):
            x = jnp.concatenate([scr[q, c * rows:(c + 1) * rows, :], scr[q + 1, c * rows:(c + 1) * rows, :]],
                                axis=1).astype(BF)
            y = jnp.dot(perm, x, preferred_element_type=F32).astype(o_ref.dtype)
            for half in range(2):
                for il in range(8):
                    o_ref[(q + half) * 8 + il, c * GATE_TG:(c + 1) * GATE_TG, :] = (
                        y[il * GATE_TG:(il + 1) * GATE_TG, half * nk:(half + 1) * nk])


def _peer_gates(i1, i2, g, perm, tm):
    t = i1.shape[0]
    row = pl.BlockSpec((tm, PEER_SEL), lambda i: (i, 0))
    return pl.pallas_call(
        _peer_gates_kernel,
        grid=(t // tm,),
        in_specs=[row, row, row, pl.BlockSpec(perm.shape, lambda i: (0, 0))],
        out_specs=pl.BlockSpec((PEER_NKEYS, tm, PEER_NKEYS), lambda i: (0, i, 0)),
        out_shape=jax.ShapeDtypeStruct((PEER_NKEYS, t, PEER_NKEYS), BF),
        scratch_shapes=[pltpu.VMEM((PEER_NKEYS // 8, tm * 8, PEER_NKEYS), F32)],
        compiler_params=_cparams(("arbitrary",)),
        name="peer_gates",
    )(i1, i2, g, perm)


PEER_EB = 8


def _peer_dense_kernel(h2_ref, x1_ref, g2_ref, gm_ref, ut_ref, v_ref, o_ref, acc_sc):
    e = pl.program_id(1)

    @pl.when(e == 0)
    def _():
        acc_sc[...] = jnp.zeros(acc_sc.shape, F32)

    h2 = h2_ref[...]
    cw = 2 * PEER_NKEYS
    ws = []
    for c in range(PEER_EB // 2):
        a = jnp.dot(h2, ut_ref[:, c * cw:(c + 1) * cw], preferred_element_type=F32)
        gate = jnp.concatenate([gm_ref[2 * c], gm_ref[2 * c + 1]], axis=1).astype(F32)
        ws.append((gate * _gelu(a)).astype(BF))
    w = jnp.concatenate(ws, axis=1)
    acc_sc[...] += jnp.dot(w, v_ref[...], preferred_element_type=F32)

    @pl.when(e == pl.num_programs(1) - 1)
    def _():
        o_ref[...] = x1_ref[...] + g2_ref[0] * acc_sc[...]


def _peer_dense(h2, x1, g2, gmat, ut_b, v_b, tm, rows_per_mod):
    t = h2.shape[0]
    ne = v_b.shape[0]
    eb = PEER_EB * PEER_NKEYS
    rg = g2.shape[1]
    return pl.pallas_call(
        _peer_dense_kernel,
        grid=(t // tm, ne // eb),
        in_specs=[pl.BlockSpec((tm, D_MODEL), lambda i, e: (i, 0)),
                  pl.BlockSpec((tm, D_MODEL), lambda i, e: (i, 0)),
                  pl.BlockSpec((1, rg, D_MODEL), lambda i, e: ((i * tm) // rows_per_mod, 0, 0)),
                  pl.BlockSpec((PEER_EB, tm, PEER_NKEYS), lambda i, e: (e, i, 0)),
                  pl.BlockSpec((D_MODEL, eb), lambda i, e: (0, e)),
                  pl.BlockSpec((eb, D_MODEL), lambda i, e: (e, 0))],
        out_specs=pl.BlockSpec((tm, D_MODEL), lambda i, e: (i, 0)),
        out_shape=jax.ShapeDtypeStruct((t, D_MODEL), F32),
        scratch_shapes=[pltpu.VMEM((tm, D_MODEL), F32)],
        compiler_params=_cparams(("arbitrary", "arbitrary")),
        name="peer_dense",
    )(h2, x1, g2, gmat, ut_b, v_b)


def _rope_tables(pos, width=256):
    inv = 1.0 / (ROPE_THETA ** (jnp.arange(0, ATT_DH, 2, dtype=F32) / ATT_DH))
    ang = pos.astype(F32)[:, None] * inv[None, :]
    cos = jnp.cos(ang)
    sin = jnp.sin(ang)
    reps = width // ATT_DH
    cos_t = jnp.tile(jnp.concatenate([cos, cos], axis=1), (1, reps))
    sin_t = jnp.tile(jnp.concatenate([-sin, sin], axis=1), (1, reps))
    return cos_t, sin_t


def _block_diag_ones(width, seg):
    r = jnp.arange(width) // seg
    return (r[:, None] == r[None, :]).astype(BF)


def _peer(h2, x1, g2, wqT, sk, u_b, v_b, tm_topk, tm_gates, tm_dense, rows_per_mod):
    i1t, i2t, gt = _peer_topk(h2, wqT, sk, tm_topk)
    r_out = jnp.arange(GATE_TG * 8)
    src = (r_out % GATE_TG) * 8 + r_out // GATE_TG
    perm = (jnp.arange(GATE_TG * 8)[None, :] == src[:, None]).astype(BF)
    gmat = _peer_gates(i1t.T, i2t.T, gt.T, perm, tm_gates)
    return _peer_dense(h2, x1, g2, gmat, u_b, v_b, tm_dense, rows_per_mod)


def kernel(x_prompt, x_sample, cache_k, cache_v, page_table, c_prompt, c_sample, w_ada, b_ada, norm1_g, w_in, q_norm_g, k_norm_g, lambda_q1, lambda_k1, lambda_q2, lambda_k2, subln_g, gm_v_norm_g, gm_ws, gm_bs, w_att_out, w_gm_out, w_out, norm2_g, peer_w_q, peer_subkeys, peer_u, peer_v):
    depth = w_ada.shape[0]
    assert depth == 1
    l = 0
    lam_init = 0.8 - 0.6 * math.exp(-0.3 * l)
    bsz, t_p, d = x_prompt.shape
    nb, t_s, _ = x_sample.shape
    n_pool, page = cache_k.shape[1], cache_k.shape[2]
    past_len = page_table.shape[1] * page

    w_in_b = w_in[l].astype(BF)
    wa_b = w_att_out[l].astype(BF)
    wb_b = w_gm_out[l].astype(BF)
    wo_b = w_out[l].astype(BF)
    wqT_b = peer_w_q[l].T.astype(BF)
    sk_b = peer_subkeys[l].reshape(2 * PEER_HEADS, PEER_NKEYS, -1).astype(BF)
    u_b = peer_u[l].T.astype(BF)
    v_b = peer_v[l].astype(BF)
    g1n = norm1_g[l].reshape(1, d)
    g2n = norm2_g[l].reshape(1, d)
    gq = jnp.tile(q_norm_g[l], 256 // ATT_DH).reshape(1, 256)
    gk = jnp.tile(k_norm_g[l], 256 // ATT_DH).reshape(1, 256)
    gv = jnp.tile(gm_v_norm_g[l], 2).reshape(1, 256)
    sg = subln_g[l].reshape(1, ATT_VD)
    lam_p = jnp.stack([lambda_q1[l], lambda_k1[l], lambda_q2[l], lambda_k2[l]], axis=0)
    seg64 = _block_diag_ones(256, ATT_DH)
    seg128 = _block_diag_ones(256, GM_WIDTH // GM_GROUPS)
    bias_p = jnp.repeat(gm_bs[l].T, GM_WIDTH // GM_GROUPS, axis=1)
    ri = jnp.arange(GM_CHUNK)
    tril_p = (ri[None, :] <= ri[:, None]).astype(F32)
    ws_s = jnp.tile(gm_ws[l][:, :t_s, :t_s], (1, GM_CHUNK // t_s, GM_CHUNK // t_s))
    mask_s = ((ri[:, None] // t_s == ri[None, :] // t_s) & (ri[None, :] % t_s <= ri[:, None] % t_s)).astype(F32)
    bias_s = jnp.tile(bias_p[:t_s], (GM_CHUNK // t_s, 1))

    n_c = bsz + nb
    c_all = jnp.concatenate([c_prompt, c_sample, jnp.zeros((-n_c % 8, d), F32)], axis=0)
    mod = _adaln(c_all, w_ada[l], b_ada[l])
    mod_p = mod[:bsz].reshape(bsz, 1, 6, d)
    mod_s = jnp.repeat(mod[bsz:n_c], t_s, axis=0).reshape(1, nb * t_s, 6, d)
    mp = [mod_p[:, :, i, :] for i in range(6)]
    ms_ = [mod_s[:, :, i, :] for i in range(6)]

    xp2 = x_prompt.reshape(bsz * t_p, d)
    cos_p, sin_p = _rope_tables(jnp.arange(t_p))
    cos_pt = jnp.tile(cos_p, (bsz, 1))
    sin_pt = jnp.tile(sin_p, (bsz, 1))
    q_p, k_p, kb_p, v_p, vb_p, u_p, vg_p, gates_p = _proj_in(
        xp2, mp[0], mp[1], g1n, w_in_b, cos_pt, sin_pt, gq, gk, gv, seg64, seg128,
        tm=512, rows_per_mod=t_p, vg_dtype=BF, k_transposed=True)
    a_p = _prompt_attention(lam_p, q_p.reshape(bsz, t_p, d), kb_p, vb_p.reshape(bsz, t_p, d), sg, lam_init)
    x1_p, h2_p = _mix(xp2, a_p.reshape(bsz * t_p, d), u_p, vg_p, gates_p, mp[2], mp[3], mp[4], g2n,
                      gm_ws[l], tril_p, bias_p, wa_b, wb_b, wo_b, tm=256, rows_per_mod=t_p)
    y_p = _peer(h2_p, x1_p, mp[5], wqT_b, sk_b, u_b, v_b,
                tm_topk=128, tm_gates=128, tm_dense=1024, rows_per_mod=t_p)

    r_s = nb * t_s
    xs2 = x_sample.reshape(r_s, d)
    cos_s, sin_s = _rope_tables(past_len + jnp.arange(t_s))
    cos_st = jnp.tile(cos_s, (nb, 1))
    sin_st = jnp.tile(sin_s, (nb, 1))
    q_s, k_s, _, v_s, _, u_s, vg_s, gates_s = _proj_in(
        xs2, ms_[0], ms_[1], g1n, w_in_b, cos_st, sin_st, gq, gk, gv, seg64, seg128,
        tm=r_s, rows_per_mod=r_s, vg_dtype=F32, k_transposed=False)
    q3 = q_s.reshape(nb, t_s, d)
    q_rep = jnp.tile(q3, (1, DEC_COLS // t_s, 1))
    rr = jnp.arange(DEC_COLS)[:, None] // t_s
    cc = jnp.arange(d)[None, :] // ATT_DH
    qbd = jnp.where(rr == cc, q_rep, jnp.zeros_like(q_rep))
    pad8 = lambda a: jnp.pad(a.reshape(nb, t_s, d), ((0, 0), (0, 8 - t_s), (0, 0)))
    cache_kt = cache_k[l].transpose(0, 2, 3, 4, 1).reshape(n_pool, d, page)
    cache_vr = cache_v[l].reshape(n_pool, page * ATT_HEADS, ATT_VD)
    a_s = _sample_attention(page_table, lam_p, qbd, pad8(k_s), pad8(v_s), sg, cache_kt, cache_vr,
                            n_q=t_s, lam_init=lam_init)
    x1_s, h2_s = _mix(xs2, a_s.reshape(r_s, d), u_s, vg_s, gates_s, ms_[2], ms_[3], ms_[4], g2n,
                      ws_s, mask_s, bias_s, wa_b, wb_b, wo_b, tm=r_s, rows_per_mod=r_s)
    y_s = _peer(h2_s, x1_s, ms_[5], wqT_b, sk_b, u_b, v_b,
                tm_topk=r_s, tm_gates=r_s, tm_dense=r_s, rows_per_mod=r_s)

    return (
        y_p.reshape(bsz, t_p, d),
        y_s.reshape(nb, t_s, d),
        k_p.reshape(1, bsz, ATT_HEADS, 2, ATT_DH, t_p).transpose(0, 1, 5, 2, 3, 4),
        v_p.reshape(1, bsz, t_p, ATT_HEADS, ATT_VD),
        k_s.reshape(1, nb, t_s, ATT_HEADS, 2, ATT_DH),
        v_s.reshape(1, nb, t_s, ATT_HEADS, ATT_VD),
        vg_s.reshape(1, nb, t_s, GM_WIDTH),
    )
```

```python
import functools
import math

import jax
import jax.numpy as jnp
from jax import lax
from jax.experimental import pallas as pl
from jax.experimental.pallas import tpu as pltpu

BF = jnp.bfloat16
F32 = jnp.float32
I32 = jnp.int32

D_MODEL = 1024
ATT_HEADS = 8
ATT_DH = 64
ATT_VD = 128
ROPE_THETA = 10000.0
GM_GROUPS = 8
GM_CHUNK = 128
GM_WIDTH = 1024
IN_WIDTH = 7168
PEER_HEADS = 8
PEER_NKEYS = 128
PEER_TOPK = 16
PEER_SEL = PEER_HEADS * PEER_TOPK
EPS = 1e-6
NEG_INF = float("-inf")

VMEM_LIMIT = 56 * 1024 * 1024


def _cparams(sem):
    return pltpu.CompilerParams(dimension_semantics=sem, vmem_limit_bytes=VMEM_LIMIT)


def _gelu(x):
    return 0.5 * x * (1.0 + lax.erf(x * math.sqrt(0.5)))


def _seg_mean_sq(z, seg, n):
    z2 = z * z
    hi = z2.astype(BF)
    lo = (z2 - hi.astype(F32)).astype(BF)
    ss = jnp.dot(hi, seg, preferred_element_type=F32) + jnp.dot(lo, seg, preferred_element_type=F32)
    return ss * (1.0 / n)


def _ada_kernel(c_ref, w_ref, b_ref, o_ref):
    c = c_ref[...]
    s = c * jax.nn.sigmoid(c)
    o_ref[...] = jnp.dot(s.astype(BF), w_ref[...].astype(BF), preferred_element_type=F32) + b_ref[...]


def _adaln(c_all, w_ada, b_ada):
    r = c_all.shape[0]
    n = w_ada.shape[1]
    tn = 1024
    return pl.pallas_call(
        _ada_kernel,
        grid=(n // tn,),
        in_specs=[
            pl.BlockSpec((r, D_MODEL), lambda j: (0, 0)),
            pl.BlockSpec((D_MODEL, tn), lambda j: (0, j)),
            pl.BlockSpec((1, tn), lambda j: (0, j)),
        ],
        out_specs=pl.BlockSpec((r, tn), lambda j: (0, j)),
        out_shape=jax.ShapeDtypeStruct((r, n), F32),
        compiler_params=_cparams(("arbitrary",)),
        name="adaln",
    )(c_all, w_ada, b_ada.reshape(1, n))


def _proj_in_kernel(x_ref, sh_ref, sc_ref, g1_ref, w_ref, cos_ref, sin_ref, gq_ref, gk_ref, gv_ref,
                    seg64_ref, seg128_ref,
                    q_ref, k_ref, kb_ref, v_ref, vb_ref, u_ref, vg_ref, gate_ref, *, k_transposed):
    x = x_ref[...]
    ms = jnp.mean(x * x, axis=-1, keepdims=True)
    h = x * lax.rsqrt(ms + EPS) * g1_ref[...]
    h = h * (1.0 + sc_ref[0]) + sh_ref[0]
    hb = h.astype(BF)
    cos = cos_ref[...]
    sin = sin_ref[...]
    seg64 = seg64_ref[...]
    seg128 = seg128_ref[...]
    tm = x.shape[0]
    cw = 256
    lane = lax.broadcasted_iota(I32, (tm, cw), 1)
    first_half = (lane % ATT_DH) < (ATT_DH // 2)

    def qk_tile(col, g):
        z = jnp.dot(hb, w_ref[:, col:col + cw], preferred_element_type=F32)
        y = z * lax.rsqrt(_seg_mean_sq(z, seg64, ATT_DH) + EPS) * g
        rot = jnp.where(first_half, pltpu.roll(y, cw - ATT_DH // 2, 1), pltpu.roll(y, ATT_DH // 2, 1))
        return y * cos + rot * sin

    for t in range(D_MODEL // cw):
        c = t * cw
        q = qk_tile(c, gq_ref[...])
        q_ref[:, c:c + cw] = (q * (ATT_DH ** -0.5)).astype(BF)
        k = qk_tile(D_MODEL + c, gk_ref[...])
        if k_transposed:
            kt = k.T
            k_ref[c:c + cw, :] = kt
            kb_ref[c:c + cw, :] = kt.astype(BF)
        else:
            k_ref[:, c:c + cw] = k
            kb_ref[:, c:c + cw] = k.astype(BF)
        v = jnp.dot(hb, w_ref[:, 2 * D_MODEL + c:2 * D_MODEL + c + cw], preferred_element_type=F32)
        v_ref[:, c:c + cw] = v
        vb_ref[:, c:c + cw] = v.astype(BF)
        zu = jnp.dot(hb, w_ref[:, 3 * D_MODEL + c:3 * D_MODEL + c + cw], preferred_element_type=F32)
        u_ref[:, c:c + cw] = _gelu(zu).astype(BF)
        zv = jnp.dot(hb, w_ref[:, 4 * D_MODEL + c:4 * D_MODEL + c + cw], preferred_element_type=F32)
        gv = _gelu(zv)
        vg = gv * lax.rsqrt(_seg_mean_sq(gv, seg128, GM_WIDTH // GM_GROUPS) + EPS) * gv_ref[...]
        vg_ref[:, c:c + cw] = vg.astype(vg_ref.dtype)
    for t in range(2 * D_MODEL // cw):
        c = t * cw
        zg = jnp.dot(hb, w_ref[:, 5 * D_MODEL + c:5 * D_MODEL + c + cw], preferred_element_type=F32)
        gate_ref[:, c:c + cw] = jax.nn.sigmoid(zg).astype(BF)


def _proj_in(x2, sh, sc, g1, w_in_b, cos, sin, gq, gk, gv, seg64, seg128, tm, rows_per_mod, vg_dtype,
             k_transposed):
    r = x2.shape[0]
    rg = sh.shape[1]
    mod_spec = pl.BlockSpec((1, rg, D_MODEL), lambda i: ((i * tm) // rows_per_mod, 0, 0))
    row = lambda w: pl.BlockSpec((tm, w), lambda i: (i, 0))
    full = lambda a: pl.BlockSpec(a.shape, lambda i: (0,) * a.ndim, pipeline_mode=pl.Buffered(1))
    if k_transposed:
        tiles = rows_per_mod // tm
        k_shape = (r // rows_per_mod, D_MODEL, rows_per_mod)
        k_spec = pl.BlockSpec((None, D_MODEL, tm), lambda i: (i // tiles, 0, i % tiles))
    else:
        k_shape = (r, D_MODEL)
        k_spec = row(D_MODEL)
    out_shapes = [
        jax.ShapeDtypeStruct((r, D_MODEL), BF),
        jax.ShapeDtypeStruct(k_shape, F32),
        jax.ShapeDtypeStruct(k_shape, BF),
        jax.ShapeDtypeStruct((r, D_MODEL), F32),
        jax.ShapeDtypeStruct((r, D_MODEL), BF),
        jax.ShapeDtypeStruct((r, D_MODEL), BF),
        jax.ShapeDtypeStruct((r, D_MODEL), vg_dtype),
        jax.ShapeDtypeStruct((r, 2 * D_MODEL), BF),
    ]
    return pl.pallas_call(
        functools.partial(_proj_in_kernel, k_transposed=k_transposed),
        grid=(r // tm,),
        in_specs=[row(D_MODEL), mod_spec, mod_spec, full(g1), full(w_in_b), row(256), row(256),
                  full(gq), full(gk), full(gv), full(seg64), full(seg128)],
        out_specs=[row(D_MODEL), k_spec, k_spec] + [row(D_MODEL)] * 4 + [row(2 * D_MODEL)],
        out_shape=out_shapes,
        compiler_params=_cparams(("arbitrary",)),
        name="proj_in",
    )(x2, sh, sc, g1, w_in_b, cos, sin, gq, gk, gv, seg64, seg128)


def _lambda_value(lam_ref, lam_init):
    lp = lam_ref[...]
    a = jnp.sum(lp[0:1] * lp[1:2], axis=-1, keepdims=True)
    b = jnp.sum(lp[2:3] * lp[3:4], axis=-1, keepdims=True)
    return jnp.exp(a) - jnp.exp(b) + lam_init


def _subln(o, sg, lam_init):
    ss = jnp.mean(o * o, axis=-1, keepdims=True)
    return o * lax.rsqrt(ss + EPS) * sg * (1.0 - lam_init)


def _attn_kernel(lam_ref, q_ref, k_ref, v_ref, sg_ref, o_ref, vx_sc, m_sc, acc_sc, *, tq, lam_init):
    qi = pl.program_id(2)
    t = v_ref.shape[0]

    @pl.when(qi == 0)
    def _():
        vx_sc[:, :ATT_VD] = v_ref[...]
        vx_sc[:, ATT_VD:] = jnp.ones((t, ATT_VD), BF)

    q = q_ref[...]
    lane = lax.broadcasted_iota(I32, q.shape, 1)
    zero = jnp.zeros_like(q)
    q2 = jnp.concatenate([jnp.where(lane < ATT_DH, q, zero), jnp.where(lane >= ATT_DH, q, zero)], axis=0)
    m_sc[...] = jnp.full(m_sc.shape, NEG_INF, F32)
    acc_sc[...] = jnp.zeros(acc_sc.shape, F32)

    def step(j, masked):
        start = pl.multiple_of(j * tq, tq)
        s = jnp.dot(q2, k_ref[:, pl.ds(start, tq)], preferred_element_type=F32)
        if masked:
            r = lax.broadcasted_iota(I32, s.shape, 0)
            c = lax.broadcasted_iota(I32, s.shape, 1)
            qpos = jnp.where(r >= tq, r - tq, r)
            s = jnp.where(c <= qpos, s, NEG_INF)
        m_prev = m_sc[...]
        m_new = jnp.maximum(m_prev, jnp.max(s, axis=-1, keepdims=True))
        alpha = jnp.exp(m_prev - m_new)
        p = jnp.exp(s - jnp.tile(m_new, (1, tq // ATT_VD)))
        pv = jnp.dot(p.astype(BF), vx_sc[pl.ds(start, tq), :], preferred_element_type=F32)
        acc_sc[...] = jnp.tile(alpha, (1, 2)) * acc_sc[...] + pv
        m_sc[...] = m_new

    def body(j, carry):
        step(j, False)
        return carry

    lax.fori_loop(0, qi, body, 0)
    step(qi, True)

    lam = _lambda_value(lam_ref, lam_init)
    o_all = acc_sc[:, :ATT_VD] / acc_sc[:, ATT_VD:]
    o = o_all[:tq] - lam * o_all[tq:]
    o_ref[...] = _subln(o, sg_ref[...], lam_init).astype(o_ref.dtype)


def _prompt_attention(lam_p, qb, kbt, vb, sg, lam_init, tq=512):
    b, t, _ = qb.shape
    kern = functools.partial(_attn_kernel, tq=tq, lam_init=lam_init)
    return pl.pallas_call(
        kern,
        grid=(b, ATT_HEADS, t // tq),
        in_specs=[
            pl.BlockSpec(lam_p.shape, lambda bi, h, i: (0, 0)),
            pl.BlockSpec((None, tq, ATT_VD), lambda bi, h, i: (bi, i, h)),
            pl.BlockSpec((None, ATT_VD, t), lambda bi, h, i: (bi, h, 0)),
            pl.BlockSpec((None, t, ATT_VD), lambda bi, h, i: (bi, 0, h)),
            pl.BlockSpec(sg.shape, lambda bi, h, i: (0, 0)),
        ],
        out_specs=pl.BlockSpec((None, tq, ATT_VD), lambda bi, h, i: (bi, i, h)),
        out_shape=jax.ShapeDtypeStruct((b, t, ATT_HEADS * ATT_VD), BF),
        scratch_shapes=[pltpu.VMEM((t, 2 * ATT_VD), BF), pltpu.VMEM((2 * tq, ATT_VD), F32),
                        pltpu.VMEM((2 * tq, 2 * ATT_VD), F32)],
        compiler_params=_cparams(("arbitrary", "arbitrary", "arbitrary")),
        name="prompt_attn",
    )(lam_p, qb, kbt, vb, sg)


DEC_PP = 8
DEC_COLS = 64


def _dec_attn_kernel(pt_ref, lam_ref, qbd_ref, kn_ref, vn_ref, sg_ref, exp_ref, hm_ref, *rest, n_q, lam_init):
    k_refs = rest[:DEC_PP]
    v_refs = rest[DEC_PP:2 * DEC_PP]
    o_ref = rest[2 * DEC_PP]
    m_sc, l_sc, acc_sc = rest[2 * DEC_PP + 1:]
    g = pl.program_id(1)
    qbd = qbd_ref[0]
    rows = lax.broadcasted_iota(I32, (DEC_COLS, ATT_VD), 0)

    @pl.when(g == 0)
    def _():
        m_sc[...] = jnp.full(m_sc.shape, NEG_INF, F32)
        l_sc[...] = jnp.zeros(l_sc.shape, F32)
        acc_sc[...] = jnp.zeros(acc_sc.shape, F32)

    def rescale(s_list):
        m_prev = m_sc[...]
        m_new = m_prev
        for s in s_list:
            m_new = jnp.maximum(m_new, jnp.max(s, axis=-1, keepdims=True))
        alpha = jnp.exp(m_prev - m_new)
        m_sc[...] = m_new
        return m_new, alpha * l_sc[...], alpha * acc_sc[...]

    s_list = [jnp.dot(qbd, k_refs[j][0].astype(BF), preferred_element_type=F32) for j in range(DEC_PP)]
    m_new, l_new, acc = rescale(s_list)
    for j in range(DEC_PP):
        p = jnp.exp(s_list[j] - m_new)
        l_new = l_new + jnp.sum(p, axis=-1, keepdims=True)
        pe = jnp.dot(p.astype(BF), exp_ref[...], preferred_element_type=F32) * hm_ref[...]
        acc = acc + jnp.dot(pe.astype(BF), v_refs[j][0].astype(BF), preferred_element_type=F32)
    l_sc[...] = l_new
    acc_sc[...] = acc

    @pl.when(g == pl.num_programs(1) - 1)
    def _():
        kn = kn_ref[0].astype(BF)
        s = lax.dot_general(qbd, kn, (((1,), (1,)), ((), ())), preferred_element_type=F32)
        r = lax.broadcasted_iota(I32, s.shape, 0)
        c = lax.broadcasted_iota(I32, s.shape, 1)
        s = jnp.where(c <= (r % n_q), s, NEG_INF)
        m_fin, l_fin, acc_fin = rescale([s])
        p = jnp.exp(s - m_fin)
        l_fin = l_fin + jnp.sum(p, axis=-1, keepdims=True)
        pv = jnp.dot(p.astype(BF), vn_ref[0].astype(BF), preferred_element_type=F32)
        for h in range(ATT_HEADS):
            own = (rows // (2 * n_q)) == h
            acc_fin = acc_fin + jnp.where(own, pv[:, h * ATT_VD:(h + 1) * ATT_VD], 0.0)
        lam = _lambda_value(lam_ref, lam_init)
        o_all = acc_fin / l_fin
        for h in range(ATT_HEADS):
            blk = o_all[h * 2 * n_q:(h + 1) * 2 * n_q]
            o = blk[:n_q] - lam * blk[n_q:]
            o_ref[0, :, h * ATT_VD:(h + 1) * ATT_VD] = _subln(o, sg_ref[...], lam_init).astype(o_ref.dtype)


def _sample_attention(page_table, lam_p, qbd, kn, vn, sg, cache_kt, cache_vr, n_q, lam_init):
    nb, n_pages = page_table.shape
    width = cache_kt.shape[1]
    page = cache_kt.shape[2]
    col = jnp.arange(page * ATT_HEADS)
    expand = (col[None, :] // ATT_HEADS == jnp.arange(page)[:, None]).astype(BF)
    hmask = (col[None, :] % ATT_HEADS == jnp.arange(DEC_COLS)[:, None] // (2 * n_q)).astype(F32)

    def page_spec(arr, j):
        return pl.BlockSpec((1,) + arr.shape[1:], lambda b, g, pt: (pt[b, g * DEC_PP + j], 0, 0))

    const = lambda a: pl.BlockSpec(a.shape, lambda b, g, pt: (0,) * a.ndim)
    kern = functools.partial(_dec_attn_kernel, n_q=n_q, lam_init=lam_init)
    grid_spec = pltpu.PrefetchScalarGridSpec(
        num_scalar_prefetch=1,
        grid=(nb, n_pages // DEC_PP),
        in_specs=[
            const(lam_p),
            pl.BlockSpec((1, DEC_COLS, width), lambda b, g, pt: (b, 0, 0)),
            pl.BlockSpec((1, 8, width), lambda b, g, pt: (b, 0, 0)),
            pl.BlockSpec((1, 8, width), lambda b, g, pt: (b, 0, 0)),
            const(sg), const(expand), const(hmask),
        ] + [page_spec(cache_kt, j) for j in range(DEC_PP)] + [page_spec(cache_vr, j) for j in range(DEC_PP)],
        out_specs=pl.BlockSpec((1, n_q, width), lambda b, g, pt: (b, 0, 0)),
        scratch_shapes=[pltpu.VMEM((DEC_COLS, 1), F32), pltpu.VMEM((DEC_COLS, 1), F32),
                        pltpu.VMEM((DEC_COLS, ATT_VD), F32)],
    )
    return pl.pallas_call(
        kern,
        grid_spec=grid_spec,
        out_shape=jax.ShapeDtypeStruct((nb, n_q, width), BF),
        compiler_params=_cparams(("arbitrary", "arbitrary")),
        name="sample_attn",
    )(page_table, lam_p, qbd, kn, vn, sg, expand, hmask, *([cache_kt] * DEC_PP), *([cache_vr] * DEC_PP))


def _mix_kernel(x_ref, a_ref, u_ref, vg_ref, gate_ref, g1_ref, sh2_ref, sc2_ref, n2_ref,
                ws_ref, wmask_ref, bias_ref, wa_ref, wb_ref, wo_ref, x1_ref, h2_ref):
    tm = x_ref.shape[0]
    gc = GM_WIDTH // GM_GROUPS
    wmask = wmask_ref[...]
    bias = bias_ref[...]
    s_chunks = []
    for c in range(tm // GM_CHUNK):
        rows = slice(c * GM_CHUNK, (c + 1) * GM_CHUNK)
        cols = []
        for g in range(GM_GROUPS):
            w = (ws_ref[g] * wmask).astype(BF)
            vv = vg_ref[rows, g * gc:(g + 1) * gc].astype(BF)
            cols.append(jnp.dot(w, vv, preferred_element_type=F32))
        mixed = jnp.concatenate(cols, axis=1) + bias
        s_chunks.append((u_ref[rows, :].astype(F32) * mixed).astype(BF))
    s_out = jnp.concatenate(s_chunks, axis=0) if len(s_chunks) > 1 else s_chunks[0]
    ya = jnp.dot(a_ref[...], wa_ref[...], preferred_element_type=F32)
    yb = jnp.dot(s_out, wb_ref[...], preferred_element_type=F32)
    ga = gate_ref[:, :D_MODEL].astype(F32)
    gb = gate_ref[:, D_MODEL:].astype(F32)
    mix = (ga * ya + gb * yb).astype(BF)
    y = jnp.dot(mix, wo_ref[...], preferred_element_type=F32)
    x1 = x_ref[...] + g1_ref[0] * y
    x1_ref[...] = x1
    ms = jnp.mean(x1 * x1, axis=-1, keepdims=True)
    h2 = x1 * lax.rsqrt(ms + EPS) * n2_ref[...]
    h2 = h2 * (1.0 + sc2_ref[0]) + sh2_ref[0]
    h2_ref[...] = h2.astype(BF)


def _mix(x2, a_out, u, vg, gates, g1, sh2, sc2, n2, ws_exp, wmask, bias_full, wa, wb, wo, tm, rows_per_mod):
    r = x2.shape[0]
    rg = g1.shape[1]
    mod_spec = pl.BlockSpec((1, rg, D_MODEL), lambda i: ((i * tm) // rows_per_mod, 0, 0))
    row = lambda w: pl.BlockSpec((tm, w), lambda i: (i, 0))
    full = lambda a: pl.BlockSpec(a.shape, lambda i: (0,) * a.ndim)
    return pl.pallas_call(
        _mix_kernel,
        grid=(r // tm,),
        in_specs=[row(D_MODEL), row(D_MODEL), row(D_MODEL), row(D_MODEL), row(2 * D_MODEL),
                  mod_spec, mod_spec, mod_spec, full(n2), full(ws_exp), full(wmask), full(bias_full),
                  full(wa), full(wb), full(wo)],
        out_specs=[row(D_MODEL), row(D_MODEL)],
        out_shape=[jax.ShapeDtypeStruct((r, D_MODEL), F32), jax.ShapeDtypeStruct((r, D_MODEL), BF)],
        compiler_params=_cparams(("arbitrary",)),
        name="mix",
    )(x2, a_out, u, vg, gates, g1, sh2, sc2, n2, ws_exp, wmask, bias_full, wa, wb, wo)


def _top_rows(s, iota, n_rows, count, val_ref, idx_ref):
    for r in range(count):
        m = jnp.max(s, axis=0, keepdims=True)
        idx = jnp.min(jnp.where(s == m, iota, n_rows), axis=0, keepdims=True)
        val_ref[r:r + 1, :] = m
        idx_ref[r:r + 1, :] = idx
        s = jnp.where(iota == idx, NEG_INF, s)


def _pair_candidates(v1, j1, v2, j2, sub8):
    nk = PEER_NKEYS
    v2_16, j2_16 = v2[...], j2[...]
    v2_8, j2_8 = v2[0:8, :], j2[0:8, :]
    cand = [v1[0:1, :] + v2_16]
    cidx = [j1[0:1, :] * nk + j2_16]
    for a in range(1, 8):
        ok = sub8 < (PEER_TOPK // (a + 1))
        cand.append(jnp.where(ok, v1[a:a + 1, :] + v2_8, NEG_INF))
        cidx.append(j1[a:a + 1, :] * nk + j2_8)
    cand.append(v1[8:16, :] + v2[0:1, :])
    cidx.append(j1[8:16, :] * nk + j2[0:1, :])
    return jnp.concatenate(cand, axis=0), jnp.concatenate(cidx, axis=0)


PEER_NCAND = 80
PEER_HEADS_PER_ITER = 8


def _peer_topk_kernel(h2_ref, wq_ref, sk_ref, i1_ref, i2_ref, g_ref, qT_sc, val_sc, idx_sc, ts_sc, eid_sc):
    tm = h2_ref.shape[0]
    nk = PEER_NKEYS
    qT_sc[...] = lax.dot_general(wq_ref[...], h2_ref[...], (((1,), (1,)), ((), ())),
                                 preferred_element_type=F32).astype(BF)
    iota_k = lax.broadcasted_iota(I32, (nk, tm), 0).astype(F32)
    iota_c = lax.broadcasted_iota(I32, (PEER_NCAND, tm), 0).astype(F32)
    sub8 = lax.broadcasted_iota(I32, (8, tm), 0)

    def body(it, carry):
        for dh in range(PEER_HEADS_PER_ITER):
            for p in range(2):
                hp = (it * PEER_HEADS_PER_ITER + dh) * 2 + p
                start = pl.multiple_of(hp * nk, nk)
                s = jnp.dot(sk_ref[hp], qT_sc[pl.ds(start, nk), :], preferred_element_type=F32)
                _top_rows(s, iota_k, nk, PEER_TOPK, val_sc.at[2 * dh + p], idx_sc.at[2 * dh + p])
        for dh in range(PEER_HEADS_PER_ITER):
            cand, cidx = _pair_candidates(val_sc.at[2 * dh], idx_sc.at[2 * dh],
                                          val_sc.at[2 * dh + 1], idx_sc.at[2 * dh + 1], sub8)
            for r in range(PEER_TOPK):
                m = jnp.max(cand, axis=0, keepdims=True)
                pos = jnp.min(jnp.where(cand == m, iota_c, float(PEER_NCAND)), axis=0, keepdims=True)
                hit = iota_c == pos
                eid_sc[dh, r:r + 1, :] = jnp.max(jnp.where(hit, cidx, -1.0), axis=0, keepdims=True)
                ts_sc[dh, r:r + 1, :] = m
                cand = jnp.where(hit, NEG_INF, cand)
            ts = ts_sc[dh]
            e = jnp.exp(ts - ts[0:1, :])
            gates = e / jnp.sum(e, axis=0, keepdims=True)
            eid = eid_sc[dh].astype(I32)
            row0 = pl.multiple_of((it * PEER_HEADS_PER_ITER + dh) * PEER_TOPK, PEER_TOPK)
            i1_ref[pl.ds(row0, PEER_TOPK), :] = lax.shift_right_logical(eid, 7).astype(F32)
            i2_ref[pl.ds(row0, PEER_TOPK), :] = lax.bitwise_and(eid, nk - 1).astype(F32)
            g_ref[pl.ds(row0, PEER_TOPK), :] = gates
        return carry

    lax.fori_loop(0, PEER_HEADS // PEER_HEADS_PER_ITER, body, 0)


def _peer_topk(h2, wqT, sk, tm):
    t = h2.shape[0]
    nq = wqT.shape[0]
    out = jax.ShapeDtypeStruct((PEER_SEL, t), F32)
    col = pl.BlockSpec((PEER_SEL, tm), lambda i: (0, i))
    return pl.pallas_call(
        _peer_topk_kernel,
        grid=(t // tm,),
        in_specs=[pl.BlockSpec((tm, D_MODEL), lambda i: (i, 0)),
                  pl.BlockSpec(wqT.shape, lambda i: (0, 0)),
                  pl.BlockSpec(sk.shape, lambda i: (0, 0, 0))],
        out_specs=[col, col, col],
        out_shape=[out, out, out],
        scratch_shapes=[pltpu.VMEM((nq, tm), BF),
                        pltpu.VMEM((2 * PEER_HEADS_PER_ITER, PEER_TOPK, tm), F32),
                        pltpu.VMEM((2 * PEER_HEADS_PER_ITER, PEER_TOPK, tm), F32),
                        pltpu.VMEM((PEER_HEADS_PER_ITER, PEER_TOPK, tm), F32),
                        pltpu.VMEM((PEER_HEADS_PER_ITER, PEER_TOPK, tm), F32)],
        compiler_params=_cparams(("arbitrary",)),
        name="peer_topk",
    )(h2, wqT, sk)


GATE_TG = 32


def _peer_gates_kernel(i1_ref, i2_ref, g_ref, perm_ref, o_ref, scr):
    tm = i1_ref.shape[0]
    nk = PEER_NKEYS
    ng = nk // 8
    iota = lax.broadcasted_iota(I32, (nk, PEER_SEL), 0).astype(F32)

    def body(t, carry):
        r1 = i1_ref[pl.ds(t, 1), :]
        r2 = i2_ref[pl.ds(t, 1), :]
        gg = g_ref[pl.ds(t, 1), :]
        a = jnp.where(iota == r1, gg, 0.0).astype(BF)
        b = jnp.where(iota == r2, 1.0, 0.0).astype(BF)
        gt = lax.dot_general(a, b, (((1,), (1,)), ((), ())), preferred_element_type=F32)
        row0 = pl.multiple_of(t * 8, 8)
        for q in range(ng):
            scr[q, pl.ds(row0, 8), :] = gt[q * 8:(q + 1) * 8, :]
        return carry

    lax.fori_loop(0, tm, body, 0, unroll=GATE_TG)
    perm = perm_ref[...]
    rows = GATE_TG * 8
    for q in range(0, ng, 2):
        for c in range(tm // GATE_TG):
            x = jnp.concatenate([scr[q, c * rows:(c + 1) * rows, :], scr[q + 1, c * rows:(c + 1) * rows, :]],
                                axis=1).astype(BF)
            y = jnp.dot(perm, x, preferred_element_type=F32).astype(o_ref.dtype)
            for half in range(2):
                for il in range(8):
                    o_ref[(q + half) * 8 + il, c * GATE_TG:(c + 1) * GATE_TG, :] = (
                        y[il * GATE_TG:(il + 1) * GATE_TG, half * nk:(half + 1) * nk])


def _peer_gates(i1, i2, g, perm, tm):
    t = i1.shape[0]
    row = pl.BlockSpec((tm, PEER_SEL), lambda i: (i, 0))
    return pl.pallas_call(
        _peer_gates_kernel,
        grid=(t // tm,),
        in_specs=[row, row, row, pl.BlockSpec(perm.shape, lambda i: (0, 0))],
        out_specs=pl.BlockSpec((PEER_NKEYS, tm, PEER_NKEYS), lambda i: (0, i, 0)),
        out_shape=jax.ShapeDtypeStruct((PEER_NKEYS, t, PEER_NKEYS), BF),
        scratch_shapes=[pltpu.VMEM((PEER_NKEYS // 8, tm * 8, PEER_NKEYS), F32)],
        compiler_params=_cparams(("arbitrary",)),
        name="peer_gates",
    )(i1, i2, g, perm)


PEER_EB = 8


def _peer_dense_kernel(h2_ref, x1_ref, g2_ref, gm_ref, ut_ref, v_ref, o_ref, acc_sc):
    e = pl.program_id(1)

    @pl.when(e == 0)
    def _():
        acc_sc[...] = jnp.zeros(acc_sc.shape, F32)

    h2 = h2_ref[...]
    cw = 2 * PEER_NKEYS
    ws = []
    for c in range(PEER_EB // 2):
        a = jnp.dot(h2, ut_ref[:, c * cw:(c + 1) * cw], preferred_element_type=F32)
        gate = jnp.concatenate([gm_ref[2 * c], gm_ref[2 * c + 1]], axis=1).astype(F32)
        ws.append((gate * _gelu(a)).astype(BF))
    w = jnp.concatenate(ws, axis=1)
    acc_sc[...] += jnp.dot(w, v_ref[...], preferred_element_type=F32)

    @pl.when(e == pl.num_programs(1) - 1)
    def _():
        o_ref[...] = x1_ref[...] + g2_ref[0] * acc_sc[...]


def _peer_dense(h2, x1, g2, gmat, ut_b, v_b, tm, rows_per_mod):
    t = h2.shape[0]
    ne = v_b.shape[0]
    eb = PEER_EB * PEER_NKEYS
    rg = g2.shape[1]
    return pl.pallas_call(
        _peer_dense_kernel,
        grid=(t // tm, ne // eb),
        in_specs=[pl.BlockSpec((tm, D_MODEL), lambda i, e: (i, 0)),
                  pl.BlockSpec((tm, D_MODEL), lambda i, e: (i, 0)),
                  pl.BlockSpec((1, rg, D_MODEL), lambda i, e: ((i * tm) // rows_per_mod, 0, 0)),
                  pl.BlockSpec((PEER_EB, tm, PEER_NKEYS), lambda i, e: (e, i, 0)),
                  pl.BlockSpec((D_MODEL, eb), lambda i, e: (0, e)),
                  pl.BlockSpec((eb, D_MODEL), lambda i, e: (e, 0))],
        out_specs=pl.BlockSpec((tm, D_MODEL), lambda i, e: (i, 0)),
        out_shape=jax.ShapeDtypeStruct((t, D_MODEL), F32),
        scratch_shapes=[pltpu.VMEM((tm, D_MODEL), F32)],
        compiler_params=_cparams(("arbitrary", "arbitrary")),
        name="peer_dense",
    )(h2, x1, g2, gmat, ut_b, v_b)


def _rope_tables(pos, width=256):
    inv = 1.0 / (ROPE_THETA ** (jnp.arange(0, ATT_DH, 2, dtype=F32) / ATT_DH))
    ang = pos.astype(F32)[:, None] * inv[None, :]
    cos = jnp.cos(ang)
    sin = jnp.sin(ang)
    reps = width // ATT_DH
    cos_t = jnp.tile(jnp.concatenate([cos, cos], axis=1), (1, reps))
    sin_t = jnp.tile(jnp.concatenate([-sin, sin], axis=1), (1, reps))
    return cos_t, sin_t


def _block_diag_ones(width, seg):
    r = jnp.arange(width) // seg
    return (r[:, None] == r[None, :]).astype(BF)


def _peer(h2, x1, g2, wqT, sk, u_b, v_b, tm_topk, tm_gates, tm_dense, rows_per_mod):
    i1t, i2t, gt = _peer_topk(h2, wqT, sk, tm_topk)
    r_out = jnp.arange(GATE_TG * 8)
    src = (r_out % GATE_TG) * 8 + r_out // GATE_TG
    perm = (jnp.arange(GATE_TG * 8)[None, :] == src[:, None]).astype(BF)
    gmat = _peer_gates(i1t.T, i2t.T, gt.T, perm, tm_gates)
    return _peer_dense(h2, x1, g2, gmat, u_b, v_b, tm_dense, rows_per_mod)


def kernel(x_prompt, x_sample, cache_k, cache_v, page_table, c_prompt, c_sample, w_ada, b_ada, norm1_g, w_in, q_norm_g, k_norm_g, lambda_q1, lambda_k1, lambda_q2, lambda_k2, subln_g, gm_v_norm_g, gm_ws, gm_bs, w_att_out, w_gm_out, w_out, norm2_g, peer_w_q, peer_subkeys, peer_u, peer_v):
    depth = w_ada.shape[0]
    assert depth == 1
    l = 0
    lam_init = 0.8 - 0.6 * math.exp(-0.3 * l)
    bsz, t_p, d = x_prompt.shape
    nb, t_s, _ = x_sample.shape
    n_pool, page = cache_k.shape[1], cache_k.shape[2]
    past_len = page_table.shape[1] * page

    w_in_b = w_in[l].astype(BF)
    wa_b = w_att_out[l].astype(BF)
    wb_b = w_gm_out[l].astype(BF)
    wo_b = w_out[l].astype(BF)
    wqT_b = peer_w_q[l].T.astype(BF)
    sk_b = peer_subkeys[l].reshape(2 * PEER_HEADS, PEER_NKEYS, -1).astype(BF)
    u_b = peer_u[l].T.astype(BF)
    v_b = peer_v[l].astype(BF)
    g1n = norm1_g[l].reshape(1, d)
    g2n = norm2_g[l].reshape(1, d)
    gq = jnp.tile(q_norm_g[l], 256 // ATT_DH).reshape(1, 256)
    gk = jnp.tile(k_norm_g[l], 256 // ATT_DH).reshape(1, 256)
    gv = jnp.tile(gm_v_norm_g[l], 2).reshape(1, 256)
    sg = subln_g[l].reshape(1, ATT_VD)
    lam_p = jnp.stack([lambda_q1[l], lambda_k1[l], lambda_q2[l], lambda_k2[l]], axis=0)
    seg64 = _block_diag_ones(256, ATT_DH)
    seg128 = _block_diag_ones(256, GM_WIDTH // GM_GROUPS)
    bias_p = jnp.repeat(gm_bs[l].T, GM_WIDTH // GM_GROUPS, axis=1)
    ri = jnp.arange(GM_CHUNK)
    tril_p = (ri[None, :] <= ri[:, None]).astype(F32)
    ws_s = jnp.tile(gm_ws[l][:, :t_s, :t_s], (1, GM_CHUNK // t_s, GM_CHUNK // t_s))
    mask_s = ((ri[:, None] // t_s == ri[None, :] // t_s) & (ri[None, :] % t_s <= ri[:, None] % t_s)).astype(F32)
    bias_s = jnp.tile(bias_p[:t_s], (GM_CHUNK // t_s, 1))

    n_c = bsz + nb
    c_all = jnp.concatenate([c_prompt, c_sample, jnp.zeros((-n_c % 8, d), F32)], axis=0)
    mod = _adaln(c_all, w_ada[l], b_ada[l])
    mod_p = mod[:bsz].reshape(bsz, 1, 6, d)
    mod_s = jnp.repeat(mod[bsz:n_c], t_s, axis=0).reshape(1, nb * t_s, 6, d)
    mp = [mod_p[:, :, i, :] for i in range(6)]
    ms_ = [mod_s[:, :, i, :] for i in range(6)]

    xp2 = x_prompt.reshape(bsz * t_p, d)
    cos_p, sin_p = _rope_tables(jnp.arange(t_p))
    cos_pt = jnp.tile(cos_p, (bsz, 1))
    sin_pt = jnp.tile(sin_p, (bsz, 1))
    q_p, k_p, kb_p, v_p, vb_p, u_p, vg_p, gates_p = _proj_in(
        xp2, mp[0], mp[1], g1n, w_in_b, cos_pt, sin_pt, gq, gk, gv, seg64, seg128,
        tm=512, rows_per_mod=t_p, vg_dtype=BF, k_transposed=True)
    a_p = _prompt_attention(lam_p, q_p.reshape(bsz, t_p, d), kb_p, vb_p.reshape(bsz, t_p, d), sg, lam_init)
    x1_p, h2_p = _mix(xp2, a_p.reshape(bsz * t_p, d), u_p, vg_p, gates_p, mp[2], mp[3], mp[4], g2n,
                      gm_ws[l], tril_p, bias_p, wa_b, wb_b, wo_b, tm=256, rows_per_mod=t_p)
    y_p = _peer(h2_p, x1_p, mp[5], wqT_b, sk_b, u_b, v_b,
                tm_topk=128, tm_gates=128, tm_dense=1024, rows_per_mod=t_p)

    r_s = nb * t_s
    xs2 = x_sample.reshape(r_s, d)
    cos_s, sin_s = _rope_tables(past_len + jnp.arange(t_s))
    cos_st = jnp.tile(cos_s, (nb, 1))
    sin_st = jnp.tile(sin_s, (nb, 1))
    q_s, k_s, _, v_s, _, u_s, vg_s, gates_s = _proj_in(
        xs2, ms_[0], ms_[1], g1n, w_in_b, cos_st, sin_st, gq, gk, gv, seg64, seg128,
        tm=r_s, rows_per_mod=r_s, vg_dtype=F32, k_transposed=False)
    q3 = q_s.reshape(nb, t_s, d)
    q_rep = jnp.tile(q3, (1, DEC_COLS // t_s, 1))
    rr = jnp.arange(DEC_COLS)[:, None] // t_s
    cc = jnp.arange(d)[None, :] // ATT_DH
    qbd = jnp.where(rr == cc, q_rep, jnp.zeros_like(q_rep))
    pad8 = lambda a: jnp.pad(a.reshape(nb, t_s, d), ((0, 0), (0, 8 - t_s), (0, 0)))
    cache_kt = cache_k[l].transpose(0, 2, 3, 4, 1).reshape(n_pool, d, page)
    cache_vr = cache_v[l].reshape(n_pool, page * ATT_HEADS, ATT_VD)
    a_s = _sample_attention(page_table, lam_p, qbd, pad8(k_s), pad8(v_s), sg, cache_kt, cache_vr,
                            n_q=t_s, lam_init=lam_init)
    x1_s, h2_s = _mix(xs2, a_s.reshape(r_s, d), u_s, vg_s, gates_s, ms_[2], ms_[3], ms_[4], g2n,
                      ws_s, mask_s, bias_s, wa_b, wb_b, wo_b, tm=r_s, rows_per_mod=r_s)
    y_s = _peer(h2_s, x1_s, ms_[5], wqT_b, sk_b, u_b, v_b,
                tm_topk=r_s, tm_gates=r_s, tm_dense=r_s, rows_per_mod=r_s)

    return (
        y_p.reshape(bsz, t_p, d),
        y_s.reshape(nb, t_s, d),
        k_p.reshape(1, bsz, ATT_HEADS, 2, ATT_DH, t_p).transpose(0, 1, 5, 2, 3, 4),
        v_p.reshape(1, bsz, t_p, ATT_HEADS, ATT_VD),
        k_s.reshape(1, nb, t_s, ATT_HEADS, 2, ATT_DH),
        v_s.reshape(1, nb, t_s, ATT_HEADS, ATT_VD),
        vg_s.reshape(1, nb, t_s, GM_WIDTH),
    )
```

```python
import functools
import math

import jax
import jax.numpy as jnp
from jax import lax
from jax.experimental import pallas as pl
from jax.experimental.pallas import tpu as pltpu

BF = jnp.bfloat16
F32 = jnp.float32
I32 = jnp.int32

D_MODEL = 1024
ATT_HEADS = 8
ATT_DH = 64
ATT_VD = 128
ROPE_THETA = 10000.0
GM_GROUPS = 8
GM_CHUNK = 128
GM_WIDTH = 1024
IN_WIDTH = 7168
PEER_HEADS = 8
PEER_NKEYS = 128
PEER_TOPK = 16
PEER_SEL = PEER_HEADS * PEER_TOPK
EPS = 1e-6
NEG_INF = float("-inf")

VMEM_LIMIT = 56 * 1024 * 1024


def _cparams(sem):
    return pltpu.CompilerParams(dimension_semantics=sem, vmem_limit_bytes=VMEM_LIMIT)


def _gelu(x):
    return 0.5 * x * (1.0 + lax.erf(x * math.sqrt(0.5)))


def _seg_mean_sq(z, seg, n):
    z2 = z * z
    hi = z2.astype(BF)
    lo = (z2 - hi.astype(F32)).astype(BF)
    ss = jnp.dot(hi, seg, preferred_element_type=F32) + jnp.dot(lo, seg, preferred_element_type=F32)
    return ss * (1.0 / n)


def _ada_kernel(c_ref, w_ref, b_ref, o_ref):
    c = c_ref[...]
    s = c * jax.nn.sigmoid(c)
    o_ref[...] = jnp.dot(s.astype(BF), w_ref[...].astype(BF), preferred_element_type=F32) + b_ref[...]


def _adaln(c_all, w_ada, b_ada):
    r = c_all.shape[0]
    n = w_ada.shape[1]
    tn = 1024
    return pl.pallas_call(
        _ada_kernel,
        grid=(n // tn,),
        in_specs=[
            pl.BlockSpec((r, D_MODEL), lambda j: (0, 0)),
            pl.BlockSpec((D_MODEL, tn), lambda j: (0, j)),
            pl.BlockSpec((1, tn), lambda j: (0, j)),
        ],
        out_specs=pl.BlockSpec((r, tn), lambda j: (0, j)),
        out_shape=jax.ShapeDtypeStruct((r, n), F32),
        compiler_params=_cparams(("arbitrary",)),
        name="adaln",
    )(c_all, w_ada, b_ada.reshape(1, n))


def _proj_in_kernel(x_ref, sh_ref, sc_ref, g1_ref, w_ref, cos_ref, sin_ref, gq_ref, gk_ref, gv_ref,
                    seg64_ref, seg128_ref,
                    q_ref, k_ref, kb_ref, v_ref, vb_ref, u_ref, vg_ref, gate_ref, *, k_transposed):
    x = x_ref[...]
    ms = jnp.mean(x * x, axis=-1, keepdims=True)
    h = x * lax.rsqrt(ms + EPS) * g1_ref[...]
    h = h * (1.0 + sc_ref[0]) + sh_ref[0]
    hb = h.astype(BF)
    cos = cos_ref[...]
    sin = sin_ref[...]
    seg64 = seg64_ref[...]
    seg128 = seg128_ref[...]
    tm = x.shape[0]
    cw = 256
    lane = lax.broadcasted_iota(I32, (tm, cw), 1)
    first_half = (lane % ATT_DH) < (ATT_DH // 2)

    def qk_tile(col, g):
        z = jnp.dot(hb, w_ref[:, col:col + cw], preferred_element_type=F32)
        y = z * lax.rsqrt(_seg_mean_sq(z, seg64, ATT_DH) + EPS) * g
        rot = jnp.where(first_half, pltpu.roll(y, cw - ATT_DH // 2, 1), pltpu.roll(y, ATT_DH // 2, 1))
        return y * cos + rot * sin

    for t in range(D_MODEL // cw):
        c = t * cw
        q = qk_tile(c, gq_ref[...])
        q_ref[:, c:c + cw] = (q * (ATT_DH ** -0.5)).astype(BF)
        k = qk_tile(D_MODEL + c, gk_ref[...])
        if k_transposed:
            kt = k.T
            k_ref[c:c + cw, :] = kt
            kb_ref[c:c + cw, :] = kt.astype(BF)
        else:
            k_ref[:, c:c + cw] = k
            kb_ref[:, c:c + cw] = k.astype(BF)
        v = jnp.dot(hb, w_ref[:, 2 * D_MODEL + c:2 * D_MODEL + c + cw], preferred_element_type=F32)
        v_ref[:, c:c + cw] = v
        vb_ref[:, c:c + cw] = v.astype(BF)
        zu = jnp.dot(hb, w_ref[:, 3 * D_MODEL + c:3 * D_MODEL + c + cw], preferred_element_type=F32)
        u_ref[:, c:c + cw] = _gelu(zu).astype(BF)
        zv = jnp.dot(hb, w_ref[:, 4 * D_MODEL + c:4 * D_MODEL + c + cw], preferred_element_type=F32)
        gv = _gelu(zv)
        vg = gv * lax.rsqrt(_seg_mean_sq(gv, seg128, GM_WIDTH // GM_GROUPS) + EPS) * gv_ref[...]
        vg_ref[:, c:c + cw] = vg.astype(vg_ref.dtype)
    for t in range(2 * D_MODEL // cw):
        c = t * cw
        zg = jnp.dot(hb, w_ref[:, 5 * D_MODEL + c:5 * D_MODEL + c + cw], preferred_element_type=F32)
        gate_ref[:, c:c + cw] = jax.nn.sigmoid(zg).astype(BF)


def _proj_in(x2, sh, sc, g1, w_in_b, cos, sin, gq, gk, gv, seg64, seg128, tm, rows_per_mod, vg_dtype,
             k_transposed):
    r = x2.shape[0]
    rg = sh.shape[1]
    mod_spec = pl.BlockSpec((1, rg, D_MODEL), lambda i: ((i * tm) // rows_per_mod, 0, 0))
    row = lambda w: pl.BlockSpec((tm, w), lambda i: (i, 0))
    full = lambda a: pl.BlockSpec(a.shape, lambda i: (0,) * a.ndim, pipeline_mode=pl.Buffered(1))
    rope_tiles = cos.shape[0] // tm
    rope_spec = pl.BlockSpec((tm, cos.shape[1]), lambda i: (i % rope_tiles, 0))
    if k_transposed:
        tiles = rows_per_mod // tm
        k_shape = (r // rows_per_mod, D_MODEL, rows_per_mod)
        k_spec = pl.BlockSpec((None, D_MODEL, tm), lambda i: (i // tiles, 0, i % tiles))
    else:
        k_shape = (r, D_MODEL)
        k_spec = row(D_MODEL)
    out_shapes = [
        jax.ShapeDtypeStruct((r, D_MODEL), BF),
        jax.ShapeDtypeStruct(k_shape, F32),
        jax.ShapeDtypeStruct(k_shape, BF),
        jax.ShapeDtypeStruct((r, D_MODEL), F32),
        jax.ShapeDtypeStruct((r, D_MODEL), BF),
        jax.ShapeDtypeStruct((r, D_MODEL), BF),
        jax.ShapeDtypeStruct((r, D_MODEL), vg_dtype),
        jax.ShapeDtypeStruct((r, 2 * D_MODEL), BF),
    ]
    return pl.pallas_call(
        functools.partial(_proj_in_kernel, k_transposed=k_transposed),
        grid=(r // tm,),
        in_specs=[row(D_MODEL), mod_spec, mod_spec, full(g1), full(w_in_b), rope_spec, rope_spec,
                  full(gq), full(gk), full(gv), full(seg64), full(seg128)],
        out_specs=[row(D_MODEL), k_spec, k_spec] + [row(D_MODEL)] * 4 + [row(2 * D_MODEL)],
        out_shape=out_shapes,
        compiler_params=_cparams(("arbitrary",)),
        name="proj_in",
    )(x2, sh, sc, g1, w_in_b, cos, sin, gq, gk, gv, seg64, seg128)


def _lambda_value(lam_ref, lam_init):
    lp = lam_ref[...]
    a = jnp.sum(lp[0:1] * lp[1:2], axis=-1, keepdims=True)
    b = jnp.sum(lp[2:3] * lp[3:4], axis=-1, keepdims=True)
    return jnp.exp(a) - jnp.exp(b) + lam_init


def _subln(o, sg, lam_init):
    ss = jnp.mean(o * o, axis=-1, keepdims=True)
    return o * lax.rsqrt(ss + EPS) * sg * (1.0 - lam_init)


def _attn_kernel(lam_ref, q_ref, k_ref, v_ref, sg_ref, o_ref, vx_sc, m_sc, acc_sc, *, tq, lam_init):
    qi = pl.program_id(2)
    t = v_ref.shape[0]

    @pl.when(qi == 0)
    def _():
        vx_sc[:, :ATT_VD] = v_ref[...]
        vx_sc[:, ATT_VD:] = jnp.ones((t, ATT_VD), BF)

    q = q_ref[...]
    lane = lax.broadcasted_iota(I32, q.shape, 1)
    zero = jnp.zeros_like(q)
    q2 = jnp.concatenate([jnp.where(lane < ATT_DH, q, zero), jnp.where(lane >= ATT_DH, q, zero)], axis=0)
    m_sc[...] = jnp.full(m_sc.shape, NEG_INF, F32)
    acc_sc[...] = jnp.zeros(acc_sc.shape, F32)

    def step(j, masked):
        start = pl.multiple_of(j * tq, tq)
        s = jnp.dot(q2, k_ref[:, pl.ds(start, tq)], preferred_element_type=F32)
        if masked:
            r = lax.broadcasted_iota(I32, s.shape, 0)
            c = lax.broadcasted_iota(I32, s.shape, 1)
            qpos = jnp.where(r >= tq, r - tq, r)
            s = jnp.where(c <= qpos, s, NEG_INF)
        m_prev = m_sc[...]
        m_new = jnp.maximum(m_prev, jnp.max(s, axis=-1, keepdims=True))
        alpha = jnp.exp(m_prev - m_new)
        p = jnp.exp(s - jnp.tile(m_new, (1, tq // ATT_VD)))
        pv = jnp.dot(p.astype(BF), vx_sc[pl.ds(start, tq), :], preferred_element_type=F32)
        acc_sc[...] = jnp.tile(alpha, (1, 2)) * acc_sc[...] + pv
        m_sc[...] = m_new

    def body(j, carry):
        step(j, False)
        return carry

    lax.fori_loop(0, qi, body, 0)
    step(qi, True)

    lam = _lambda_value(lam_ref, lam_init)
    o_all = acc_sc[:, :ATT_VD] / acc_sc[:, ATT_VD:]
    o = o_all[:tq] - lam * o_all[tq:]
    o_ref[...] = _subln(o, sg_ref[...], lam_init).astype(o_ref.dtype)


def _prompt_attention(lam_p, qb, kbt, vb, sg, lam_init, tq=512):
    b, t, _ = qb.shape
    kern = functools.partial(_attn_kernel, tq=tq, lam_init=lam_init)
    return pl.pallas_call(
        kern,
        grid=(b, ATT_HEADS, t // tq),
        in_specs=[
            pl.BlockSpec(lam_p.shape, lambda bi, h, i: (0, 0)),
            pl.BlockSpec((None, tq, ATT_VD), lambda bi, h, i: (bi, i, h)),
            pl.BlockSpec((None, ATT_VD, t), lambda bi, h, i: (bi, h, 0)),
            pl.BlockSpec((None, t, ATT_VD), lambda bi, h, i: (bi, 0, h)),
            pl.BlockSpec(sg.shape, lambda bi, h, i: (0, 0)),
        ],
        out_specs=pl.BlockSpec((None, tq, ATT_VD), lambda bi, h, i: (bi, i, h)),
        out_shape=jax.ShapeDtypeStruct((b, t, ATT_HEADS * ATT_VD), BF),
        scratch_shapes=[pltpu.VMEM((t, 2 * ATT_VD), BF), pltpu.VMEM((2 * tq, ATT_VD), F32),
                        pltpu.VMEM((2 * tq, 2 * ATT_VD), F32)],
        compiler_params=_cparams(("arbitrary", "arbitrary", "arbitrary")),
        name="prompt_attn",
    )(lam_p, qb, kbt, vb, sg)


DEC_PP = 8
DEC_COLS = 64


def _dec_attn_kernel(pt_ref, lam_ref, qbd_ref, kn_ref, vn_ref, sg_ref, exp_ref, hm_ref, *rest, n_q, lam_init):
    k_refs = rest[:DEC_PP]
    v_refs = rest[DEC_PP:2 * DEC_PP]
    o_ref = rest[2 * DEC_PP]
    kcat_sc, m_sc, l_sc, acc_sc = rest[2 * DEC_PP + 1:]
    g = pl.program_id(1)
    qbd = qbd_ref[0]
    rows = lax.broadcasted_iota(I32, (DEC_COLS, ATT_VD), 0)
    page = k_refs[0].shape[2]

    @pl.when(g == 0)
    def _():
        m_sc[...] = jnp.full(m_sc.shape, NEG_INF, F32)
        l_sc[...] = jnp.zeros(l_sc.shape, F32)
        acc_sc[...] = jnp.zeros(acc_sc.shape, F32)

    def rescale(s_list):
        m_prev = m_sc[...]
        m_new = m_prev
        for s in s_list:
            m_new = jnp.maximum(m_new, jnp.max(s, axis=-1, keepdims=True))
        alpha = jnp.exp(m_prev - m_new)
        m_sc[...] = m_new
        return m_new, alpha * l_sc[...], alpha * acc_sc[...]

    for j in range(DEC_PP):
        kcat_sc[:, j * page:(j + 1) * page] = k_refs[j][0].astype(BF)
    s = jnp.dot(qbd, kcat_sc[...], preferred_element_type=F32)
    m_new, l_new, acc = rescale([s])
    p = jnp.exp(s - m_new)
    l_sc[...] = l_new + jnp.sum(p, axis=-1, keepdims=True)
    p_rows = jnp.concatenate([p[:, j * page:(j + 1) * page] for j in range(DEC_PP)], axis=0).astype(BF)
    pe = jnp.dot(p_rows, exp_ref[...], preferred_element_type=F32).astype(BF) * hm_ref[...]
    for j in range(DEC_PP):
        acc = acc + jnp.dot(pe[j * DEC_COLS:(j + 1) * DEC_COLS], v_refs[j][0].astype(BF),
                            preferred_element_type=F32)
    acc_sc[...] = acc

    @pl.when(g == pl.num_programs(1) - 1)
    def _():
        kn = kn_ref[0].astype(BF)
        s = lax.dot_general(qbd, kn, (((1,), (1,)), ((), ())), preferred_element_type=F32)
        r = lax.broadcasted_iota(I32, s.shape, 0)
        c = lax.broadcasted_iota(I32, s.shape, 1)
        s = jnp.where(c <= (r % n_q), s, NEG_INF)
        m_fin, l_fin, acc_fin = rescale([s])
        p = jnp.exp(s - m_fin)
        l_fin = l_fin + jnp.sum(p, axis=-1, keepdims=True)
        pv = jnp.dot(p.astype(BF), vn_ref[0].astype(BF), preferred_element_type=F32)
        for h in range(ATT_HEADS):
            own = (rows // (2 * n_q)) == h
            acc_fin = acc_fin + jnp.where(own, pv[:, h * ATT_VD:(h + 1) * ATT_VD], 0.0)
        lam = _lambda_value(lam_ref, lam_init)
        o_all = acc_fin / l_fin
        for h in range(ATT_HEADS):
            blk = o_all[h * 2 * n_q:(h + 1) * 2 * n_q]
            o = blk[:n_q] - lam * blk[n_q:]
            o_ref[0, :, h * ATT_VD:(h + 1) * ATT_VD] = _subln(o, sg_ref[...], lam_init).astype(o_ref.dtype)


def _sample_attention(page_table, lam_p, qbd, kn, vn, sg, cache_kt, cache_vr, n_q, lam_init):
    nb, n_pages = page_table.shape
    width = cache_kt.shape[1]
    page = cache_kt.shape[2]
    col = jnp.arange(page * ATT_HEADS)
    expand = (col[None, :] // ATT_HEADS == jnp.arange(page)[:, None]).astype(BF)
    row_head = (jnp.arange(DEC_PP * DEC_COLS) % DEC_COLS) // (2 * n_q)
    hmask = (col[None, :] % ATT_HEADS == row_head[:, None]).astype(BF)

    def page_spec(arr, j):
        return pl.BlockSpec((1,) + arr.shape[1:], lambda b, g, pt: (pt[b, g * DEC_PP + j], 0, 0))

    const = lambda a: pl.BlockSpec(a.shape, lambda b, g, pt: (0,) * a.ndim)
    kern = functools.partial(_dec_attn_kernel, n_q=n_q, lam_init=lam_init)
    grid_spec = pltpu.PrefetchScalarGridSpec(
        num_scalar_prefetch=1,
        grid=(nb, n_pages // DEC_PP),
        in_specs=[
            const(lam_p),
            pl.BlockSpec((1, DEC_COLS, width), lambda b, g, pt: (b, 0, 0)),
            pl.BlockSpec((1, 8, width), lambda b, g, pt: (b, 0, 0)),
            pl.BlockSpec((1, 8, width), lambda b, g, pt: (b, 0, 0)),
            const(sg), const(expand), const(hmask),
        ] + [page_spec(cache_kt, j) for j in range(DEC_PP)] + [page_spec(cache_vr, j) for j in range(DEC_PP)],
        out_specs=pl.BlockSpec((1, n_q, width), lambda b, g, pt: (b, 0, 0)),
        scratch_shapes=[pltpu.VMEM((width, DEC_PP * page), BF),
                        pltpu.VMEM((DEC_COLS, 1), F32), pltpu.VMEM((DEC_COLS, 1), F32),
                        pltpu.VMEM((DEC_COLS, ATT_VD), F32)],
    )
    return pl.pallas_call(
        kern,
        grid_spec=grid_spec,
        out_shape=jax.ShapeDtypeStruct((nb, n_q, width), BF),
        compiler_params=_cparams(("arbitrary", "arbitrary")),
        name="sample_attn",
    )(page_table, lam_p, qbd, kn, vn, sg, expand, hmask, *([cache_kt] * DEC_PP), *([cache_vr] * DEC_PP))


def _mix_kernel(x_ref, a_ref, u_ref, vg_ref, gate_ref, g1_ref, sh2_ref, sc2_ref, n2_ref,
                ws_ref, wmask_ref, bias_ref, wa_ref, wb_ref, wo_ref, wq_ref, x1_ref, h2_ref, pq_ref):
    tm = x_ref.shape[0]
    gc = GM_WIDTH // GM_GROUPS
    wmask = wmask_ref[...]
    bias = bias_ref[...]
    s_chunks = []
    for c in range(tm // GM_CHUNK):
        rows = slice(c * GM_CHUNK, (c + 1) * GM_CHUNK)
        cols = []
        for g in range(GM_GROUPS):
            w = (ws_ref[g] * wmask).astype(BF)
            vv = vg_ref[rows, g * gc:(g + 1) * gc].astype(BF)
            cols.append(jnp.dot(w, vv, preferred_element_type=F32))
        mixed = jnp.concatenate(cols, axis=1) + bias
        s_chunks.append((u_ref[rows, :].astype(F32) * mixed).astype(BF))
    s_out = jnp.concatenate(s_chunks, axis=0) if len(s_chunks) > 1 else s_chunks[0]
    ya = jnp.dot(a_ref[...], wa_ref[...], preferred_element_type=F32)
    yb = jnp.dot(s_out, wb_ref[...], preferred_element_type=F32)
    ga = gate_ref[:, :D_MODEL].astype(F32)
    gb = gate_ref[:, D_MODEL:].astype(F32)
    mix = (ga * ya + gb * yb).astype(BF)
    y = jnp.dot(mix, wo_ref[...], preferred_element_type=F32)
    x1 = x_ref[...] + g1_ref[0] * y
    x1_ref[...] = x1
    ms = jnp.mean(x1 * x1, axis=-1, keepdims=True)
    h2 = x1 * lax.rsqrt(ms + EPS) * n2_ref[...]
    h2 = h2 * (1.0 + sc2_ref[0]) + sh2_ref[0]
    h2b = h2.astype(BF)
    h2_ref[...] = h2b
    pq_ref[...] = jnp.dot(h2b, wq_ref[...], preferred_element_type=F32).astype(BF)


def _mix(x2, a_out, u, vg, gates, g1, sh2, sc2, n2, ws_exp, wmask, bias_full, wa, wb, wo, wq, tm, rows_per_mod):
    r = x2.shape[0]
    rg = g1.shape[1]
    mod_spec = pl.BlockSpec((1, rg, D_MODEL), lambda i: ((i * tm) // rows_per_mod, 0, 0))
    row = lambda w: pl.BlockSpec((tm, w), lambda i: (i, 0))
    full = lambda a: pl.BlockSpec(a.shape, lambda i: (0,) * a.ndim)
    return pl.pallas_call(
        _mix_kernel,
        grid=(r // tm,),
        in_specs=[row(D_MODEL), row(D_MODEL), row(D_MODEL), row(D_MODEL), row(2 * D_MODEL),
                  mod_spec, mod_spec, mod_spec, full(n2), full(ws_exp), full(wmask), full(bias_full),
                  full(wa), full(wb), full(wo), full(wq)],
        out_specs=[row(D_MODEL), row(D_MODEL), row(wq.shape[1])],
        out_shape=[jax.ShapeDtypeStruct((r, D_MODEL), F32), jax.ShapeDtypeStruct((r, D_MODEL), BF),
                   jax.ShapeDtypeStruct((r, wq.shape[1]), BF)],
        compiler_params=_cparams(("arbitrary",)),
        name="mix",
    )(x2, a_out, u, vg, gates, g1, sh2, sc2, n2, ws_exp, wmask, bias_full, wa, wb, wo, wq)


def _top_rows(s, iota, n_rows, count, val_ref, idx_ref):
    for r in range(count):
        m = jnp.max(s, axis=0, keepdims=True)
        idx = jnp.min(jnp.where(s == m, iota, n_rows), axis=0, keepdims=True)
        val_ref[r:r + 1, :] = m
        idx_ref[r:r + 1, :] = idx
        s = jnp.where(iota == idx, NEG_INF, s)


def _pair_candidates(v1, j1, v2, j2, sub8):
    nk = PEER_NKEYS
    v2_16, j2_16 = v2[...], j2[...]
    v2_8, j2_8 = v2[0:8, :], j2[0:8, :]
    cand = [v1[0:1, :] + v2_16]
    cidx = [j1[0:1, :] * nk + j2_16]
    for a in range(1, 8):
        ok = sub8 < (PEER_TOPK // (a + 1))
        cand.append(jnp.where(ok, v1[a:a + 1, :] + v2_8, NEG_INF))
        cidx.append(j1[a:a + 1, :] * nk + j2_8)
    cand.append(v1[8:16, :] + v2[0:1, :])
    cidx.append(j1[8:16, :] * nk + j2[0:1, :])
    return jnp.concatenate(cand, axis=0), jnp.concatenate(cidx, axis=0)


PEER_NCAND = 80
def _peer_topk_kernel(q_ref, sk_ref, i1_ref, i2_ref, g_ref, val_sc, idx_sc, ts_sc, eid_sc):
    tm = q_ref.shape[0]
    nk = PEER_NKEYS
    iota_k = lax.broadcasted_iota(I32, (nk, tm), 0).astype(F32)
    iota_c = lax.broadcasted_iota(I32, (PEER_NCAND, tm), 0).astype(F32)
    sub8 = lax.broadcasted_iota(I32, (8, tm), 0)

    for hp in range(2 * PEER_HEADS):
        s = lax.dot_general(sk_ref[hp], q_ref[:, hp * nk:(hp + 1) * nk], (((1,), (1,)), ((), ())),
                            preferred_element_type=F32)
        _top_rows(s, iota_k, nk, PEER_TOPK, val_sc.at[hp], idx_sc.at[hp])
    for h in range(PEER_HEADS):
        cand, cidx = _pair_candidates(val_sc.at[2 * h], idx_sc.at[2 * h],
                                      val_sc.at[2 * h + 1], idx_sc.at[2 * h + 1], sub8)
        for r in range(PEER_TOPK):
            m = jnp.max(cand, axis=0, keepdims=True)
            pos = jnp.min(jnp.where(cand == m, iota_c, float(PEER_NCAND)), axis=0, keepdims=True)
            hit = iota_c == pos
            eid_sc[h, r:r + 1, :] = jnp.max(jnp.where(hit, cidx, -1.0), axis=0, keepdims=True)
            ts_sc[h, r:r + 1, :] = m
            cand = jnp.where(hit, NEG_INF, cand)
        ts = ts_sc[h]
        e = jnp.exp(ts - ts[0:1, :])
        gates = e / jnp.sum(e, axis=0, keepdims=True)
        eid = eid_sc[h].astype(I32)
        rows = slice(h * PEER_TOPK, (h + 1) * PEER_TOPK)
        i1_ref[rows, :] = lax.shift_right_logical(eid, 7).astype(F32)
        i2_ref[rows, :] = lax.bitwise_and(eid, nk - 1).astype(F32)
        g_ref[rows, :] = gates


def _peer_topk(q, sk, tm):
    t, nq = q.shape
    out = jax.ShapeDtypeStruct((PEER_SEL, t), F32)
    col = pl.BlockSpec((PEER_SEL, tm), lambda i: (0, i))
    return pl.pallas_call(
        _peer_topk_kernel,
        grid=(t // tm,),
        in_specs=[pl.BlockSpec((tm, nq), lambda i: (i, 0)),
                  pl.BlockSpec(sk.shape, lambda i: (0, 0, 0))],
        out_specs=[col, col, col],
        out_shape=[out, out, out],
        scratch_shapes=[pltpu.VMEM((2 * PEER_HEADS, PEER_TOPK, tm), F32),
                        pltpu.VMEM((2 * PEER_HEADS, PEER_TOPK, tm), F32),
                        pltpu.VMEM((PEER_HEADS, PEER_TOPK, tm), F32),
                        pltpu.VMEM((PEER_HEADS, PEER_TOPK, tm), F32)],
        compiler_params=_cparams(("arbitrary",)),
        name="peer_topk",
    )(q, sk)


GATE_TG = 32


def _peer_gates_kernel(i1_ref, i2_ref, g_ref, perm_ref, o_ref, scr):
    tm = i1_ref.shape[0]
    nk = PEER_NKEYS
    ng = nk // 8
    iota = lax.broadcasted_iota(I32, (nk, PEER_SEL), 0).astype(F32)

    def body(t, carry):
        r1 = i1_ref[pl.ds(t, 1), :]
        r2 = i2_ref[pl.ds(t, 1), :]
        gg = g_ref[pl.ds(t, 1), :]
        a = jnp.where(iota == r1, gg, 0.0).astype(BF)
        b = jnp.where(iota == r2, 1.0, 0.0).astype(BF)
        gt = lax.dot_general(a, b, (((1,), (1,)), ((), ())), preferred_element_type=F32)
        row0 = pl.multiple_of(t * 8, 8)
        for q in range(ng):
            scr[q, pl.ds(row0, 8), :] = gt[q * 8:(q + 1) * 8, :]
        return carry

    lax.fori_loop(0, tm, body, 0, unroll=2 * GATE_TG)
    perm = perm_ref[...]
    rows = GATE_TG * 8
    for q in range(0, ng, 2):
        for c in range(tm // GATE_TG):
            x = jnp.concatenate([scr[q, c * rows:(c + 1) * rows, :], scr[q + 1, c * rows:(c + 1) * rows, :]],
                                axis=1).astype(BF)
            y = jnp.dot(perm, x, preferred_element_type=F32).astype(o_ref.dtype)
            for half in range(2):
                for il in range(8):
                    o_ref[(q + half) * 8 + il, c * GATE_TG:(c + 1) * GATE_TG, :] = (
                        y[il * GATE_TG:(il + 1) * GATE_TG, half * nk:(half + 1) * nk])


def _peer_gates(i1, i2, g, perm, tm):
    t = i1.shape[0]
    row = pl.BlockSpec((tm, PEER_SEL), lambda i: (i, 0))
    return pl.pallas_call(
        _peer_gates_kernel,
        grid=(t // tm,),
        in_specs=[row, row, row, pl.BlockSpec(perm.shape, lambda i: (0, 0))],
        out_specs=pl.BlockSpec((PEER_NKEYS, tm, PEER_NKEYS), lambda i: (0, i, 0)),
        out_shape=jax.ShapeDtypeStruct((PEER_NKEYS, t, PEER_NKEYS), BF),
        scratch_shapes=[pltpu.VMEM((PEER_NKEYS // 8, tm * 8, PEER_NKEYS), F32)],
        compiler_params=_cparams(("arbitrary",)),
        name="peer_gates",
    )(i1, i2, g, perm)


PEER_EB = 8


def _peer_dense_kernel(h2_ref, x1_ref, g2_ref, gm_ref, ut_ref, v_ref, o_ref, acc_sc):
    e = pl.program_id(1)

    @pl.when(e == 0)
    def _():
        acc_sc[...] = jnp.zeros(acc_sc.shape, F32)

    h2 = h2_ref[...]
    cw = 2 * PEER_NKEYS
    ws = []
    for c in range(PEER_EB // 2):
        a = jnp.dot(h2, ut_ref[:, c * cw:(c + 1) * cw], preferred_element_type=F32)
        gate = jnp.concatenate([gm_ref[2 * c], gm_ref[2 * c + 1]], axis=1).astype(F32)
        ws.append((gate * _gelu(a)).astype(BF))
    w = jnp.concatenate(ws, axis=1)
    acc_sc[...] += jnp.dot(w, v_ref[...], preferred_element_type=F32)

    @pl.when(e == pl.num_programs(1) - 1)
    def _():
        o_ref[...] = x1_ref[...] + g2_ref[0] * acc_sc[...]


def _peer_dense(h2, x1, g2, gmat, ut_b, v_b, tm, rows_per_mod):
    t = h2.shape[0]
    ne = v_b.shape[0]
    eb = PEER_EB * PEER_NKEYS
    rg = g2.shape[1]
    return pl.pallas_call(
        _peer_dense_kernel,
        grid=(t // tm, ne // eb),
        in_specs=[pl.BlockSpec((tm, D_MODEL), lambda i, e: (i, 0)),
                  pl.BlockSpec((tm, D_MODEL), lambda i, e: (i, 0)),
                  pl.BlockSpec((1, rg, D_MODEL), lambda i, e: ((i * tm) // rows_per_mod, 0, 0)),
                  pl.BlockSpec((PEER_EB, tm, PEER_NKEYS), lambda i, e: (e, i, 0)),
                  pl.BlockSpec((D_MODEL, eb), lambda i, e: (0, e)),
                  pl.BlockSpec((eb, D_MODEL), lambda i, e: (e, 0))],
        out_specs=pl.BlockSpec((tm, D_MODEL), lambda i, e: (i, 0)),
        out_shape=jax.ShapeDtypeStruct((t, D_MODEL), F32),
        scratch_shapes=[pltpu.VMEM((tm, D_MODEL), F32)],
        compiler_params=_cparams(("arbitrary", "arbitrary")),
        name="peer_dense",
    )(h2, x1, g2, gmat, ut_b, v_b)


def _rope_tables(pos, width=256):
    inv = 1.0 / (ROPE_THETA ** (jnp.arange(0, ATT_DH, 2, dtype=F32) / ATT_DH))
    ang = pos.astype(F32)[:, None] * inv[None, :]
    cos = jnp.cos(ang)
    sin = jnp.sin(ang)
    reps = width // ATT_DH
    cos_t = jnp.tile(jnp.concatenate([cos, cos], axis=1), (1, reps))
    sin_t = jnp.tile(jnp.concatenate([-sin, sin], axis=1), (1, reps))
    return cos_t, sin_t


def _block_diag_ones(width, seg):
    r = jnp.arange(width) // seg
    return (r[:, None] == r[None, :]).astype(BF)


def _peer(h2, pq, x1, g2, sk, u_b, v_b, tm_topk, tm_gates, tm_dense, rows_per_mod):
    i1t, i2t, gt = _peer_topk(pq, sk, tm_topk)
    r_out = jnp.arange(GATE_TG * 8)
    src = (r_out % GATE_TG) * 8 + r_out // GATE_TG
    perm = (jnp.arange(GATE_TG * 8)[None, :] == src[:, None]).astype(BF)
    gmat = _peer_gates(i1t.T, i2t.T, gt.T, perm, tm_gates)
    return _peer_dense(h2, x1, g2, gmat, u_b, v_b, tm_dense, rows_per_mod)


def kernel(x_prompt, x_sample, cache_k, cache_v, page_table, c_prompt, c_sample, w_ada, b_ada, norm1_g, w_in, q_norm_g, k_norm_g, lambda_q1, lambda_k1, lambda_q2, lambda_k2, subln_g, gm_v_norm_g, gm_ws, gm_bs, w_att_out, w_gm_out, w_out, norm2_g, peer_w_q, peer_subkeys, peer_u, peer_v):
    depth = w_ada.shape[0]
    assert depth == 1
    l = 0
    lam_init = 0.8 - 0.6 * math.exp(-0.3 * l)
    bsz, t_p, d = x_prompt.shape
    nb, t_s, _ = x_sample.shape
    n_pool, page = cache_k.shape[1], cache_k.shape[2]
    past_len = page_table.shape[1] * page

    w_in_b = w_in[l].astype(BF)
    wa_b = w_att_out[l].astype(BF)
    wb_b = w_gm_out[l].astype(BF)
    wo_b = w_out[l].astype(BF)
    wq_b = peer_w_q[l].astype(BF)
    sk_b = peer_subkeys[l].reshape(2 * PEER_HEADS, PEER_NKEYS, -1).astype(BF)
    u_b = peer_u[l].astype(BF).T
    v_b = peer_v[l].astype(BF)
    g1n = norm1_g[l].reshape(1, d)
    g2n = norm2_g[l].reshape(1, d)
    gq = jnp.tile(q_norm_g[l], 256 // ATT_DH).reshape(1, 256)
    gk = jnp.tile(k_norm_g[l], 256 // ATT_DH).reshape(1, 256)
    gv = jnp.tile(gm_v_norm_g[l], 2).reshape(1, 256)
    sg = subln_g[l].reshape(1, ATT_VD)
    lam_p = jnp.stack([lambda_q1[l], lambda_k1[l], lambda_q2[l], lambda_k2[l]], axis=0)
    seg64 = _block_diag_ones(256, ATT_DH)
    seg128 = _block_diag_ones(256, GM_WIDTH // GM_GROUPS)
    bias_p = jnp.repeat(gm_bs[l].T, GM_WIDTH // GM_GROUPS, axis=1)
    ri = jnp.arange(GM_CHUNK)
    tril_p = (ri[None, :] <= ri[:, None]).astype(F32)
    ws_s = jnp.tile(gm_ws[l][:, :t_s, :t_s], (1, GM_CHUNK // t_s, GM_CHUNK // t_s))
    mask_s = ((ri[:, None] // t_s == ri[None, :] // t_s) & (ri[None, :] % t_s <= ri[:, None] % t_s)).astype(F32)
    bias_s = jnp.tile(bias_p[:t_s], (GM_CHUNK // t_s, 1))

    n_c = bsz + nb
    c_all = jnp.concatenate([c_prompt, c_sample, jnp.zeros((-n_c % 8, d), F32)], axis=0)
    mod = _adaln(c_all, w_ada[l], b_ada[l])
    mod_p = mod[:bsz].reshape(bsz, 1, 6, d)
    mod_s = jnp.repeat(mod[bsz:n_c], t_s, axis=0).reshape(1, nb * t_s, 6, d)
    mp = [mod_p[:, :, i, :] for i in range(6)]
    ms_ = [mod_s[:, :, i, :] for i in range(6)]

    xp2 = x_prompt.reshape(bsz * t_p, d)
    cos_p, sin_p = _rope_tables(jnp.arange(t_p))
    q_p, k_p, kb_p, v_p, vb_p, u_p, vg_p, gates_p = _proj_in(
        xp2, mp[0], mp[1], g1n, w_in_b, cos_p, sin_p, gq, gk, gv, seg64, seg128,
        tm=512, rows_per_mod=t_p, vg_dtype=BF, k_transposed=True)
    a_p = _prompt_attention(lam_p, q_p.reshape(bsz, t_p, d), kb_p, vb_p.reshape(bsz, t_p, d), sg, lam_init)
    x1_p, h2_p, pq_p = _mix(xp2, a_p.reshape(bsz * t_p, d), u_p, vg_p, gates_p, mp[2], mp[3], mp[4], g2n,
                            gm_ws[l], tril_p, bias_p, wa_b, wb_b, wo_b, wq_b, tm=256, rows_per_mod=t_p)
    y_p = _peer(h2_p, pq_p, x1_p, mp[5], sk_b, u_b, v_b,
                tm_topk=128, tm_gates=128, tm_dense=1024, rows_per_mod=t_p)

    r_s = nb * t_s
    xs2 = x_sample.reshape(r_s, d)
    cos_s, sin_s = _rope_tables(past_len + jnp.arange(t_s))
    cos_st = jnp.tile(cos_s, (nb, 1))
    sin_st = jnp.tile(sin_s, (nb, 1))
    q_s, k_s, _, v_s, _, u_s, vg_s, gates_s = _proj_in(
        xs2, ms_[0], ms_[1], g1n, w_in_b, cos_st, sin_st, gq, gk, gv, seg64, seg128,
        tm=r_s, rows_per_mod=r_s, vg_dtype=F32, k_transposed=False)
    q3 = q_s.reshape(nb, t_s, d)
    q_rep = jnp.tile(q3, (1, DEC_COLS // t_s, 1))
    rr = jnp.arange(DEC_COLS)[:, None] // t_s
    cc = jnp.arange(d)[None, :] // ATT_DH
    qbd = jnp.where(rr == cc, q_rep, jnp.zeros_like(q_rep))
    pad8 = lambda a: jnp.pad(a.reshape(nb, t_s, d), ((0, 0), (0, 8 - t_s), (0, 0)))
    cache_kt = cache_k[l].transpose(0, 2, 3, 4, 1).reshape(n_pool, d, page)
    cache_vr = cache_v[l].reshape(n_pool, page * ATT_HEADS, ATT_VD)
    a_s = _sample_attention(page_table, lam_p, qbd, pad8(k_s), pad8(v_s), sg, cache_kt, cache_vr,
                            n_q=t_s, lam_init=lam_init)
    x1_s, h2_s, pq_s = _mix(xs2, a_s.reshape(r_s, d), u_s, vg_s, gates_s, ms_[2], ms_[3], ms_[4], g2n,
                            ws_s, mask_s, bias_s, wa_b, wb_b, wo_b, wq_b, tm=r_s, rows_per_mod=r_s)
    y_s = _peer(h2_s, pq_s, x1_s, ms_[5], sk_b, u_b, v_b,
                tm_topk=r_s, tm_gates=r_s, tm_dense=r_s, rows_per_mod=r_s)

    return (
        y_p.reshape(bsz, t_p, d),
        y_s.reshape(nb, t_s, d),
        k_p.reshape(1, bsz, ATT_HEADS, 2, ATT_DH, t_p).transpose(0, 1, 5, 2, 3, 4),
        v_p.reshape(1, bsz, t_p, ATT_HEADS, ATT_VD),
        k_s.reshape(1, nb, t_s, ATT_HEADS, 2, ATT_DH),
        v_s.reshape(1, nb, t_s, ATT_HEADS, ATT_VD),
        vg_s.reshape(1, nb, t_s, GM_WIDTH),
    )
```

```python
import functools
import math

import jax
import jax.numpy as jnp
import numpy as np
from jax import lax
from jax.experimental import pallas as pl
from jax.experimental.pallas import tpu as pltpu

BF = jnp.bfloat16
F32 = jnp.float32
I32 = jnp.int32

D_MODEL = 1024
ATT_HEADS = 8
ATT_DH = 64
ATT_VD = 128
ROPE_THETA = 10000.0
GM_GROUPS = 8
GM_CHUNK = 128
GM_WIDTH = 1024
IN_WIDTH = 7168
PEER_HEADS = 8
PEER_NKEYS = 128
PEER_TOPK = 16
PEER_SEL = PEER_HEADS * PEER_TOPK
EPS = 1e-6
NEG_INF = float("-inf")

VMEM_LIMIT = 56 * 1024 * 1024


def _cparams(sem):
    return pltpu.CompilerParams(dimension_semantics=sem, vmem_limit_bytes=VMEM_LIMIT)


def _gelu(x):
    return 0.5 * x * (1.0 + lax.erf(x * math.sqrt(0.5)))


def _seg_mean_sq(z, seg, n):
    z2 = z * z
    hi = z2.astype(BF)
    lo = (z2 - hi.astype(F32)).astype(BF)
    ss = jnp.dot(hi, seg, preferred_element_type=F32) + jnp.dot(lo, seg, preferred_element_type=F32)
    return ss * (1.0 / n)


def _ada_kernel(c_ref, w_ref, b_ref, o_ref):
    c = c_ref[...]
    s = c * jax.nn.sigmoid(c)
    o_ref[...] = jnp.dot(s.astype(BF), w_ref[...].astype(BF), preferred_element_type=F32) + b_ref[...]


def _adaln(c_all, w_ada, b_ada):
    r = c_all.shape[0]
    n = w_ada.shape[1]
    tn = 1024
    return pl.pallas_call(
        _ada_kernel,
        grid=(n // tn,),
        in_specs=[
            pl.BlockSpec((r, D_MODEL), lambda j: (0, 0)),
            pl.BlockSpec((D_MODEL, tn), lambda j: (0, j)),
            pl.BlockSpec((1, tn), lambda j: (0, j)),
        ],
        out_specs=pl.BlockSpec((r, tn), lambda j: (0, j)),
        out_shape=jax.ShapeDtypeStruct((r, n), F32),
        compiler_params=_cparams(("arbitrary",)),
        name="adaln",
    )(c_all, w_ada, b_ada.reshape(1, n))


def _proj_in_kernel(x_ref, sh_ref, sc_ref, g1_ref, w_ref, cos_ref, sin_ref, gq_ref, gk_ref, gv_ref,
                    seg64_ref, seg128_ref,
                    q_ref, k_ref, kb_ref, v_ref, vb_ref, u_ref, vg_ref, gate_ref, *, k_transposed):
    x = x_ref[...]
    ms = jnp.mean(x * x, axis=-1, keepdims=True)
    h = x * lax.rsqrt(ms + EPS) * g1_ref[...]
    h = h * (1.0 + sc_ref[0]) + sh_ref[0]
    hb = h.astype(BF)
    cos = cos_ref[...]
    sin = sin_ref[...]
    seg64 = seg64_ref[...]
    seg128 = seg128_ref[...]
    tm = x.shape[0]
    cw = 256
    lane = lax.broadcasted_iota(I32, (tm, cw), 1)
    first_half = (lane % ATT_DH) < (ATT_DH // 2)

    def qk_tile(col, g):
        z = jnp.dot(hb, w_ref[:, col:col + cw], preferred_element_type=F32)
        y = z * lax.rsqrt(_seg_mean_sq(z, seg64, ATT_DH) + EPS) * g
        rot = jnp.where(first_half, pltpu.roll(y, cw - ATT_DH // 2, 1), pltpu.roll(y, ATT_DH // 2, 1))
        return y * cos + rot * sin

    for t in range(D_MODEL // cw):
        c = t * cw
        q = qk_tile(c, gq_ref[...])
        q_ref[:, c:c + cw] = (q * (ATT_DH ** -0.5)).astype(BF)
        k = qk_tile(D_MODEL + c, gk_ref[...])
        if k_transposed:
            kt = k.T
            k_ref[c:c + cw, :] = kt
            kb_ref[c:c + cw, :] = kt.astype(BF)
        else:
            k_ref[:, c:c + cw] = k
            kb_ref[:, c:c + cw] = k.astype(BF)
        v = jnp.dot(hb, w_ref[:, 2 * D_MODEL + c:2 * D_MODEL + c + cw], preferred_element_type=F32)
        v_ref[:, c:c + cw] = v
        vb_ref[:, c:c + cw] = v.astype(BF)
        zu = jnp.dot(hb, w_ref[:, 3 * D_MODEL + c:3 * D_MODEL + c + cw], preferred_element_type=F32)
        u_ref[:, c:c + cw] = _gelu(zu).astype(BF)
        zv = jnp.dot(hb, w_ref[:, 4 * D_MODEL + c:4 * D_MODEL + c + cw], preferred_element_type=F32)
        gv = _gelu(zv)
        vg = gv * lax.rsqrt(_seg_mean_sq(gv, seg128, GM_WIDTH // GM_GROUPS) + EPS) * gv_ref[...]
        vg_ref[:, c:c + cw] = vg.astype(vg_ref.dtype)
    for t in range(2 * D_MODEL // cw):
        c = t * cw
        zg = jnp.dot(hb, w_ref[:, 5 * D_MODEL + c:5 * D_MODEL + c + cw], preferred_element_type=F32)
        gate_ref[:, c:c + cw] = jax.nn.sigmoid(zg).astype(BF)


def _proj_in(x2, sh, sc, g1, w_in_b, cos, sin, gq, gk, gv, seg64, seg128, tm, rows_per_mod, vg_dtype,
             k_transposed):
    r = x2.shape[0]
    rg = sh.shape[1]
    mod_spec = pl.BlockSpec((1, rg, D_MODEL), lambda i: ((i * tm) // rows_per_mod, 0, 0))
    row = lambda w: pl.BlockSpec((tm, w), lambda i: (i, 0))
    full = lambda a: pl.BlockSpec(a.shape, lambda i: (0,) * a.ndim, pipeline_mode=pl.Buffered(1))
    rope_tiles = cos.shape[0] // tm
    rope_spec = pl.BlockSpec((tm, cos.shape[1]), lambda i: (i % rope_tiles, 0))
    if k_transposed:
        tiles = rows_per_mod // tm
        k_shape = (r // rows_per_mod, D_MODEL, rows_per_mod)
        k_spec = pl.BlockSpec((None, D_MODEL, tm), lambda i: (i // tiles, 0, i % tiles))
    else:
        k_shape = (r, D_MODEL)
        k_spec = row(D_MODEL)
    out_shapes = [
        jax.ShapeDtypeStruct((r, D_MODEL), BF),
        jax.ShapeDtypeStruct(k_shape, F32),
        jax.ShapeDtypeStruct(k_shape, BF),
        jax.ShapeDtypeStruct((r, D_MODEL), F32),
        jax.ShapeDtypeStruct((r, D_MODEL), BF),
        jax.ShapeDtypeStruct((r, D_MODEL), BF),
        jax.ShapeDtypeStruct((r, D_MODEL), vg_dtype),
        jax.ShapeDtypeStruct((r, 2 * D_MODEL), BF),
    ]
    return pl.pallas_call(
        functools.partial(_proj_in_kernel, k_transposed=k_transposed),
        grid=(r // tm,),
        in_specs=[row(D_MODEL), mod_spec, mod_spec, full(g1), full(w_in_b), rope_spec, rope_spec,
                  full(gq), full(gk), full(gv), full(seg64), full(seg128)],
        out_specs=[row(D_MODEL), k_spec, k_spec] + [row(D_MODEL)] * 4 + [row(2 * D_MODEL)],
        out_shape=out_shapes,
        compiler_params=_cparams(("arbitrary",)),
        name="proj_in",
    )(x2, sh, sc, g1, w_in_b, cos, sin, gq, gk, gv, seg64, seg128)


def _lambda_value(lam_ref, lam_init):
    lp = lam_ref[...]
    a = jnp.sum(lp[0:1] * lp[1:2], axis=-1, keepdims=True)
    b = jnp.sum(lp[2:3] * lp[3:4], axis=-1, keepdims=True)
    return jnp.exp(a) - jnp.exp(b) + lam_init


def _subln(o, sg, lam_init):
    ss = jnp.mean(o * o, axis=-1, keepdims=True)
    return o * lax.rsqrt(ss + EPS) * sg * (1.0 - lam_init)


def _attn_kernel(lam_ref, q_ref, k_ref, v_ref, sg_ref, o_ref, vx_sc, m_sc, acc_sc, *, tq, lam_init):
    qi = pl.program_id(2)
    t = v_ref.shape[0]

    @pl.when(qi == 0)
    def _():
        vx_sc[:, :ATT_VD] = v_ref[...]
        vx_sc[:, ATT_VD:] = jnp.ones((t, ATT_VD), BF)

    q = q_ref[...]
    lane = lax.broadcasted_iota(I32, q.shape, 1)
    zero = jnp.zeros_like(q)
    q2 = jnp.concatenate([jnp.where(lane < ATT_DH, q, zero), jnp.where(lane >= ATT_DH, q, zero)], axis=0)
    m_sc[...] = jnp.full(m_sc.shape, NEG_INF, F32)
    acc_sc[...] = jnp.zeros(acc_sc.shape, F32)

    def step(j, masked):
        start = pl.multiple_of(j * tq, tq)
        s = jnp.dot(q2, k_ref[:, pl.ds(start, tq)], preferred_element_type=F32)
        if masked:
            r = lax.broadcasted_iota(I32, s.shape, 0)
            c = lax.broadcasted_iota(I32, s.shape, 1)
            qpos = jnp.where(r >= tq, r - tq, r)
            s = jnp.where(c <= qpos, s, NEG_INF)
        m_prev = m_sc[...]
        m_new = jnp.maximum(m_prev, jnp.max(s, axis=-1, keepdims=True))
        alpha = jnp.exp(m_prev - m_new)
        p = jnp.exp(s - jnp.tile(m_new, (1, tq // ATT_VD)))
        pv = jnp.dot(p.astype(BF), vx_sc[pl.ds(start, tq), :], preferred_element_type=F32)
        acc_sc[...] = jnp.tile(alpha, (1, 2)) * acc_sc[...] + pv
        m_sc[...] = m_new

    def body(j, carry):
        step(j, False)
        return carry

    lax.fori_loop(0, qi, body, 0)
    step(qi, True)

    lam = _lambda_value(lam_ref, lam_init)
    o_all = acc_sc[:, :ATT_VD] / acc_sc[:, ATT_VD:]
    o = o_all[:tq] - lam * o_all[tq:]
    o_ref[...] = _subln(o, sg_ref[...], lam_init).astype(o_ref.dtype)


def _prompt_attention(lam_p, qb, kbt, vb, sg, lam_init, tq=512):
    b, t, _ = qb.shape
    kern = functools.partial(_attn_kernel, tq=tq, lam_init=lam_init)
    return pl.pallas_call(
        kern,
        grid=(b, ATT_HEADS, t // tq),
        in_specs=[
            pl.BlockSpec(lam_p.shape, lambda bi, h, i: (0, 0)),
            pl.BlockSpec((None, tq, ATT_VD), lambda bi, h, i: (bi, i, h)),
            pl.BlockSpec((None, ATT_VD, t), lambda bi, h, i: (bi, h, 0)),
            pl.BlockSpec((None, t, ATT_VD), lambda bi, h, i: (bi, 0, h)),
            pl.BlockSpec(sg.shape, lambda bi, h, i: (0, 0)),
        ],
        out_specs=pl.BlockSpec((None, tq, ATT_VD), lambda bi, h, i: (bi, i, h)),
        out_shape=jax.ShapeDtypeStruct((b, t, ATT_HEADS * ATT_VD), BF),
        scratch_shapes=[pltpu.VMEM((t, 2 * ATT_VD), BF), pltpu.VMEM((2 * tq, ATT_VD), F32),
                        pltpu.VMEM((2 * tq, 2 * ATT_VD), F32)],
        compiler_params=_cparams(("arbitrary", "arbitrary", "arbitrary")),
        name="prompt_attn",
    )(lam_p, qb, kbt, vb, sg)


DEC_PP = 8
DEC_COLS = 64


def _dec_attn_kernel(pt_ref, lam_ref, qbd_ref, kn_ref, vn_ref, sg_ref, exp_ref, hm_ref, *rest, n_q, lam_init):
    k_refs = rest[:DEC_PP]
    v_refs = rest[DEC_PP:2 * DEC_PP]
    o_ref = rest[2 * DEC_PP]
    kcat_sc, m_sc, l_sc, acc_sc = rest[2 * DEC_PP + 1:]
    g = pl.program_id(1)
    qbd = qbd_ref[0]
    rows = lax.broadcasted_iota(I32, (DEC_COLS, ATT_VD), 0)
    page = k_refs[0].shape[2]

    @pl.when(g == 0)
    def _():
        m_sc[...] = jnp.full(m_sc.shape, NEG_INF, F32)
        l_sc[...] = jnp.zeros(l_sc.shape, F32)
        acc_sc[...] = jnp.zeros(acc_sc.shape, F32)

    def rescale(s_list):
        m_prev = m_sc[...]
        m_new = m_prev
        for s in s_list:
            m_new = jnp.maximum(m_new, jnp.max(s, axis=-1, keepdims=True))
        alpha = jnp.exp(m_prev - m_new)
        m_sc[...] = m_new
        return m_new, alpha * l_sc[...], alpha * acc_sc[...]

    for j in range(DEC_PP):
        kcat_sc[:, j * page:(j + 1) * page] = k_refs[j][0].astype(BF)
    s = jnp.dot(qbd, kcat_sc[...], preferred_element_type=F32)
    m_new, l_new, acc = rescale([s])
    p = jnp.exp(s - m_new)
    l_sc[...] = l_new + jnp.sum(p, axis=-1, keepdims=True)
    p_rows = jnp.concatenate([p[:, j * page:(j + 1) * page] for j in range(DEC_PP)], axis=0).astype(BF)
    pe = jnp.dot(p_rows, exp_ref[...], preferred_element_type=F32).astype(BF) * hm_ref[...]
    for j in range(DEC_PP):
        acc = acc + jnp.dot(pe[j * DEC_COLS:(j + 1) * DEC_COLS], v_refs[j][0].astype(BF),
                            preferred_element_type=F32)
    acc_sc[...] = acc

    @pl.when(g == pl.num_programs(1) - 1)
    def _():
        kn = kn_ref[0].astype(BF)
        s = lax.dot_general(qbd, kn, (((1,), (1,)), ((), ())), preferred_element_type=F32)
        r = lax.broadcasted_iota(I32, s.shape, 0)
        c = lax.broadcasted_iota(I32, s.shape, 1)
        s = jnp.where(c <= (r % n_q), s, NEG_INF)
        m_fin, l_fin, acc_fin = rescale([s])
        p = jnp.exp(s - m_fin)
        l_fin = l_fin + jnp.sum(p, axis=-1, keepdims=True)
        pv = jnp.dot(p.astype(BF), vn_ref[0].astype(BF), preferred_element_type=F32)
        for h in range(ATT_HEADS):
            own = (rows // (2 * n_q)) == h
            acc_fin = acc_fin + jnp.where(own, pv[:, h * ATT_VD:(h + 1) * ATT_VD], 0.0)
        lam = _lambda_value(lam_ref, lam_init)
        o_all = acc_fin / l_fin
        for h in range(ATT_HEADS):
            blk = o_all[h * 2 * n_q:(h + 1) * 2 * n_q]
            o = blk[:n_q] - lam * blk[n_q:]
            o_ref[0, :, h * ATT_VD:(h + 1) * ATT_VD] = _subln(o, sg_ref[...], lam_init).astype(o_ref.dtype)


def _sample_attention(page_table, lam_p, qbd, kn, vn, sg, cache_kt, cache_vr, n_q, lam_init):
    nb, n_pages = page_table.shape
    width = cache_kt.shape[1]
    page = cache_kt.shape[2]
    col = np.arange(page * ATT_HEADS)
    expand = jnp.asarray((col[None, :] // ATT_HEADS == np.arange(page)[:, None]).astype(BF))
    row_head = (np.arange(DEC_PP * DEC_COLS) % DEC_COLS) // (2 * n_q)
    hmask = jnp.asarray((col[None, :] % ATT_HEADS == row_head[:, None]).astype(BF))

    def page_spec(arr, j):
        return pl.BlockSpec((1,) + arr.shape[1:], lambda b, g, pt: (pt[b, g * DEC_PP + j], 0, 0))

    const = lambda a: pl.BlockSpec(a.shape, lambda b, g, pt: (0,) * a.ndim)
    kern = functools.partial(_dec_attn_kernel, n_q=n_q, lam_init=lam_init)
    grid_spec = pltpu.PrefetchScalarGridSpec(
        num_scalar_prefetch=1,
        grid=(nb, n_pages // DEC_PP),
        in_specs=[
            const(lam_p),
            pl.BlockSpec((1, DEC_COLS, width), lambda b, g, pt: (b, 0, 0)),
            pl.BlockSpec((1, 8, width), lambda b, g, pt: (b, 0, 0)),
            pl.BlockSpec((1, 8, width), lambda b, g, pt: (b, 0, 0)),
            const(sg), const(expand), const(hmask),
        ] + [page_spec(cache_kt, j) for j in range(DEC_PP)] + [page_spec(cache_vr, j) for j in range(DEC_PP)],
        out_specs=pl.BlockSpec((1, n_q, width), lambda b, g, pt: (b, 0, 0)),
        scratch_shapes=[pltpu.VMEM((width, DEC_PP * page), BF),
                        pltpu.VMEM((DEC_COLS, 1), F32), pltpu.VMEM((DEC_COLS, 1), F32),
                        pltpu.VMEM((DEC_COLS, ATT_VD), F32)],
    )
    return pl.pallas_call(
        kern,
        grid_spec=grid_spec,
        out_shape=jax.ShapeDtypeStruct((nb, n_q, width), BF),
        compiler_params=_cparams(("arbitrary", "arbitrary")),
        name="sample_attn",
    )(page_table, lam_p, qbd, kn, vn, sg, expand, hmask, *([cache_kt] * DEC_PP), *([cache_vr] * DEC_PP))


def _mix_kernel(x_ref, a_ref, u_ref, vg_ref, gate_ref, g1_ref, sh2_ref, sc2_ref, n2_ref,
                ws_ref, wmask_ref, bias_ref, wa_ref, wb_ref, wo_ref, wq_ref, x1_ref, h2_ref, pq_ref):
    tm = x_ref.shape[0]
    gc = GM_WIDTH // GM_GROUPS
    wmask = wmask_ref[...]
    bias = bias_ref[...]
    s_chunks = []
    for c in range(tm // GM_CHUNK):
        rows = slice(c * GM_CHUNK, (c + 1) * GM_CHUNK)
        cols = []
        for g in range(GM_GROUPS):
            w = (ws_ref[g] * wmask).astype(BF)
            vv = vg_ref[rows, g * gc:(g + 1) * gc].astype(BF)
            cols.append(jnp.dot(w, vv, preferred_element_type=F32))
        mixed = jnp.concatenate(cols, axis=1) + bias
        s_chunks.append((u_ref[rows, :].astype(F32) * mixed).astype(BF))
    s_out = jnp.concatenate(s_chunks, axis=0) if len(s_chunks) > 1 else s_chunks[0]
    ya = jnp.dot(a_ref[...], wa_ref[...], preferred_element_type=F32)
    yb = jnp.dot(s_out, wb_ref[...], preferred_element_type=F32)
    ga = gate_ref[:, :D_MODEL].astype(F32)
    gb = gate_ref[:, D_MODEL:].astype(F32)
    mix = (ga * ya + gb * yb).astype(BF)
    y = jnp.dot(mix, wo_ref[...], preferred_element_type=F32)
    x1 = x_ref[...] + g1_ref[0] * y
    x1_ref[...] = x1
    ms = jnp.mean(x1 * x1, axis=-1, keepdims=True)
    h2 = x1 * lax.rsqrt(ms + EPS) * n2_ref[...]
    h2 = h2 * (1.0 + sc2_ref[0]) + sh2_ref[0]
    h2b = h2.astype(BF)
    h2_ref[...] = h2b
    pq_ref[...] = jnp.dot(h2b, wq_ref[...], preferred_element_type=F32).astype(BF)


def _mix(x2, a_out, u, vg, gates, g1, sh2, sc2, n2, ws_exp, wmask, bias_full, wa, wb, wo, wq, tm, rows_per_mod):
    r = x2.shape[0]
    rg = g1.shape[1]
    mod_spec = pl.BlockSpec((1, rg, D_MODEL), lambda i: ((i * tm) // rows_per_mod, 0, 0))
    row = lambda w: pl.BlockSpec((tm, w), lambda i: (i, 0))
    full = lambda a: pl.BlockSpec(a.shape, lambda i: (0,) * a.ndim)
    return pl.pallas_call(
        _mix_kernel,
        grid=(r // tm,),
        in_specs=[row(D_MODEL), row(D_MODEL), row(D_MODEL), row(D_MODEL), row(2 * D_MODEL),
                  mod_spec, mod_spec, mod_spec, full(n2), full(ws_exp), full(wmask), full(bias_full),
                  full(wa), full(wb), full(wo), full(wq)],
        out_specs=[row(D_MODEL), row(D_MODEL), row(wq.shape[1])],
        out_shape=[jax.ShapeDtypeStruct((r, D_MODEL), F32), jax.ShapeDtypeStruct((r, D_MODEL), BF),
                   jax.ShapeDtypeStruct((r, wq.shape[1]), BF)],
        compiler_params=_cparams(("arbitrary",)),
        name="mix",
    )(x2, a_out, u, vg, gates, g1, sh2, sc2, n2, ws_exp, wmask, bias_full, wa, wb, wo, wq)


def _top_rows(s, iota, n_rows, count, val_ref, idx_ref):
    for r in range(count):
        m = jnp.max(s, axis=0, keepdims=True)
        idx = jnp.min(jnp.where(s == m, iota, n_rows), axis=0, keepdims=True)
        val_ref[r:r + 1, :] = m
        idx_ref[r:r + 1, :] = idx
        s = jnp.where(iota == idx, NEG_INF, s)


def _pair_candidates(v1, j1, v2, j2, sub8):
    nk = PEER_NKEYS
    v2_16, j2_16 = v2[...], j2[...]
    v2_8, j2_8 = v2[0:8, :], j2[0:8, :]
    cand = [v1[0:1, :] + v2_16]
    cidx = [j1[0:1, :] * nk + j2_16]
    for a in range(1, 8):
        ok = sub8 < (PEER_TOPK // (a + 1))
        cand.append(jnp.where(ok, v1[a:a + 1, :] + v2_8, NEG_INF))
        cidx.append(j1[a:a + 1, :] * nk + j2_8)
    cand.append(v1[8:16, :] + v2[0:1, :])
    cidx.append(j1[8:16, :] * nk + j2[0:1, :])
    return jnp.concatenate(cand, axis=0), jnp.concatenate(cidx, axis=0)


PEER_NCAND = 80
def _peer_topk_kernel(q_ref, sk_ref, i1_ref, i2_ref, g_ref, val_sc, idx_sc, ts_sc, eid_sc):
    tm = q_ref.shape[0]
    nk = PEER_NKEYS
    iota_k = lax.broadcasted_iota(I32, (nk, tm), 0).astype(F32)
    iota_c = lax.broadcasted_iota(I32, (PEER_NCAND, tm), 0).astype(F32)
    sub8 = lax.broadcasted_iota(I32, (8, tm), 0)

    for hp in range(2 * PEER_HEADS):
        s = lax.dot_general(sk_ref[hp], q_ref[:, hp * nk:(hp + 1) * nk], (((1,), (1,)), ((), ())),
                            preferred_element_type=F32)
        _top_rows(s, iota_k, nk, PEER_TOPK, val_sc.at[hp], idx_sc.at[hp])
    for h in range(PEER_HEADS):
        cand, cidx = _pair_candidates(val_sc.at[2 * h], idx_sc.at[2 * h],
                                      val_sc.at[2 * h + 1], idx_sc.at[2 * h + 1], sub8)
        for r in range(PEER_TOPK):
            m = jnp.max(cand, axis=0, keepdims=True)
            pos = jnp.min(jnp.where(cand == m, iota_c, float(PEER_NCAND)), axis=0, keepdims=True)
            hit = iota_c == pos
            eid_sc[h, r:r + 1, :] = jnp.max(jnp.where(hit, cidx, -1.0), axis=0, keepdims=True)
            ts_sc[h, r:r + 1, :] = m
            cand = jnp.where(hit, NEG_INF, cand)
        ts = ts_sc[h]
        e = jnp.exp(ts - ts[0:1, :])
        gates = e / jnp.sum(e, axis=0, keepdims=True)
        eid = eid_sc[h].astype(I32)
        rows = slice(h * PEER_TOPK, (h + 1) * PEER_TOPK)
        i1_ref[rows, :] = lax.shift_right_logical(eid, 7).astype(F32)
        i2_ref[rows, :] = lax.bitwise_and(eid, nk - 1).astype(F32)
        g_ref[rows, :] = gates


def _peer_topk(q, sk, tm):
    t, nq = q.shape
    out = jax.ShapeDtypeStruct((PEER_SEL, t), F32)
    col = pl.BlockSpec((PEER_SEL, tm), lambda i: (0, i))
    return pl.pallas_call(
        _peer_topk_kernel,
        grid=(t // tm,),
        in_specs=[pl.BlockSpec((tm, nq), lambda i: (i, 0)),
                  pl.BlockSpec(sk.shape, lambda i: (0, 0, 0))],
        out_specs=[col, col, col],
        out_shape=[out, out, out],
        scratch_shapes=[pltpu.VMEM((2 * PEER_HEADS, PEER_TOPK, tm), F32),
                        pltpu.VMEM((2 * PEER_HEADS, PEER_TOPK, tm), F32),
                        pltpu.VMEM((PEER_HEADS, PEER_TOPK, tm), F32),
                        pltpu.VMEM((PEER_HEADS, PEER_TOPK, tm), F32)],
        compiler_params=_cparams(("arbitrary",)),
        name="peer_topk",
    )(q, sk)


GATE_TG = 32


def _peer_gates_kernel(i1_ref, i2_ref, g_ref, perm_ref, o_ref, scr):
    tm = i1_ref.shape[0]
    nk = PEER_NKEYS
    ng = nk // 8
    iota = lax.broadcasted_iota(I32, (nk, PEER_SEL), 0).astype(F32)

    def body(t, carry):
        r1 = i1_ref[pl.ds(t, 1), :]
        r2 = i2_ref[pl.ds(t, 1), :]
        gg = g_ref[pl.ds(t, 1), :]
        a = jnp.where(iota == r1, gg, 0.0).astype(BF)
        b = jnp.where(iota == r2, 1.0, 0.0).astype(BF)
        gt = lax.dot_general(a, b, (((1,), (1,)), ((), ())), preferred_element_type=F32)
        row0 = pl.multiple_of(t * 8, 8)
        for q in range(ng):
            scr[q, pl.ds(row0, 8), :] = gt[q * 8:(q + 1) * 8, :]
        return carry

    lax.fori_loop(0, tm, body, 0, unroll=2 * GATE_TG)
    perm = perm_ref[...]
    rows = GATE_TG * 8
    for q in range(0, ng, 2):
        for c in range(tm // GATE_TG):
            x = jnp.concatenate([scr[q, c * rows:(c + 1) * rows, :], scr[q + 1, c * rows:(c + 1) * rows, :]],
                                axis=1).astype(BF)
            y = jnp.dot(perm, x, preferred_element_type=F32).astype(o_ref.dtype)
            for half in range(2):
                for il in range(8):
                    o_ref[(q + half) * 8 + il, c * GATE_TG:(c + 1) * GATE_TG, :] = (
                        y[il * GATE_TG:(il + 1) * GATE_TG, half * nk:(half + 1) * nk])


def _peer_gates(i1, i2, g, perm, tm):
    t = i1.shape[0]
    row = pl.BlockSpec((tm, PEER_SEL), lambda i: (i, 0))
    return pl.pallas_call(
        _peer_gates_kernel,
        grid=(t // tm,),
        in_specs=[row, row, row, pl.BlockSpec(perm.shape, lambda i: (0, 0))],
        out_specs=pl.BlockSpec((PEER_NKEYS, tm, PEER_NKEYS), lambda i: (0, i, 0)),
        out_shape=jax.ShapeDtypeStruct((PEER_NKEYS, t, PEER_NKEYS), BF),
        scratch_shapes=[pltpu.VMEM((PEER_NKEYS // 8, tm * 8, PEER_NKEYS), F32)],
        compiler_params=_cparams(("arbitrary",)),
        name="peer_gates",
    )(i1, i2, g, perm)


PEER_EB = 16


def _peer_dense_kernel(h2_ref, x1_ref, g2_ref, gm_ref, ut_ref, v_ref, o_ref, acc_sc):
    e = pl.program_id(1)

    @pl.when(e == 0)
    def _():
        acc_sc[...] = jnp.zeros(acc_sc.shape, F32)

    h2 = h2_ref[...]
    cw = 2 * PEER_NKEYS
    ws = []
    for c in range(PEER_EB // 2):
        a = jnp.dot(h2, ut_ref[:, c * cw:(c + 1) * cw], preferred_element_type=F32)
        gate = jnp.concatenate([gm_ref[2 * c], gm_ref[2 * c + 1]], axis=1).astype(F32)
        ws.append((gate * _gelu(a)).astype(BF))
    w = jnp.concatenate(ws, axis=1)
    acc_sc[...] += jnp.dot(w, v_ref[...], preferred_element_type=F32)

    @pl.when(e == pl.num_programs(1) - 1)
    def _():
        o_ref[...] = x1_ref[...] + g2_ref[0] * acc_sc[...]


def _peer_dense(h2, x1, g2, gmat, ut_b, v_b, tm, rows_per_mod):
    t = h2.shape[0]
    ne = v_b.shape[0]
    eb = PEER_EB * PEER_NKEYS
    rg = g2.shape[1]
    return pl.pallas_call(
        _peer_dense_kernel,
        grid=(t // tm, ne // eb),
        in_specs=[pl.BlockSpec((tm, D_MODEL), lambda i, e: (i, 0)),
                  pl.BlockSpec((tm, D_MODEL), lambda i, e: (i, 0)),
                  pl.BlockSpec((1, rg, D_MODEL), lambda i, e: ((i * tm) // rows_per_mod, 0, 0)),
                  pl.BlockSpec((PEER_EB, tm, PEER_NKEYS), lambda i, e: (e, i, 0)),
                  pl.BlockSpec((D_MODEL, eb), lambda i, e: (0, e)),
                  pl.BlockSpec((eb, D_MODEL), lambda i, e: (e, 0))],
        out_specs=pl.BlockSpec((tm, D_MODEL), lambda i, e: (i, 0)),
        out_shape=jax.ShapeDtypeStruct((t, D_MODEL), F32),
        scratch_shapes=[pltpu.VMEM((tm, D_MODEL), F32)],
        compiler_params=_cparams(("arbitrary", "arbitrary")),
        name="peer_dense",
    )(h2, x1, g2, gmat, ut_b, v_b)


def _rope_tables(pos, width=256):
    inv = 1.0 / (ROPE_THETA ** (jnp.arange(0, ATT_DH, 2, dtype=F32) / ATT_DH))
    ang = pos.astype(F32)[:, None] * inv[None, :]
    cos = jnp.cos(ang)
    sin = jnp.sin(ang)
    reps = width // ATT_DH
    cos_t = jnp.tile(jnp.concatenate([cos, cos], axis=1), (1, reps))
    sin_t = jnp.tile(jnp.concatenate([-sin, sin], axis=1), (1, reps))
    return cos_t, sin_t


def _block_diag_ones(width, seg):
    r = np.arange(width) // seg
    return jnp.asarray((r[:, None] == r[None, :]).astype(BF))


def _peer(h2, pq, x1, g2, sk, u_b, v_b, tm_topk, tm_gates, tm_dense, rows_per_mod):
    i1t, i2t, gt = _peer_topk(pq, sk, tm_topk)
    r_out = np.arange(GATE_TG * 8)
    src = (r_out % GATE_TG) * 8 + r_out // GATE_TG
    perm = jnp.asarray((np.arange(GATE_TG * 8)[None, :] == src[:, None]).astype(BF))
    gmat = _peer_gates(i1t.T, i2t.T, gt.T, perm, tm_gates)
    return _peer_dense(h2, x1, g2, gmat, u_b, v_b, tm_dense, rows_per_mod)


def kernel(x_prompt, x_sample, cache_k, cache_v, page_table, c_prompt, c_sample, w_ada, b_ada, norm1_g, w_in, q_norm_g, k_norm_g, lambda_q1, lambda_k1, lambda_q2, lambda_k2, subln_g, gm_v_norm_g, gm_ws, gm_bs, w_att_out, w_gm_out, w_out, norm2_g, peer_w_q, peer_subkeys, peer_u, peer_v):
    depth = w_ada.shape[0]
    assert depth == 1
    l = 0
    lam_init = 0.8 - 0.6 * math.exp(-0.3 * l)
    bsz, t_p, d = x_prompt.shape
    nb, t_s, _ = x_sample.shape
    n_pool, page = cache_k.shape[1], cache_k.shape[2]
    past_len = page_table.shape[1] * page

    w_in_b = w_in[l].astype(BF)
    wa_b = w_att_out[l].astype(BF)
    wb_b = w_gm_out[l].astype(BF)
    wo_b = w_out[l].astype(BF)
    wq_b = peer_w_q[l].astype(BF)
    sk_b = peer_subkeys[l].reshape(2 * PEER_HEADS, PEER_NKEYS, -1).astype(BF)
    u_b = peer_u[l].astype(BF).T
    v_b = peer_v[l].astype(BF)
    g1n = norm1_g[l].reshape(1, d)
    g2n = norm2_g[l].reshape(1, d)
    gq = jnp.tile(q_norm_g[l], 256 // ATT_DH).reshape(1, 256)
    gk = jnp.tile(k_norm_g[l], 256 // ATT_DH).reshape(1, 256)
    gv = jnp.tile(gm_v_norm_g[l], 2).reshape(1, 256)
    sg = subln_g[l].reshape(1, ATT_VD)
    lam_p = jnp.stack([lambda_q1[l], lambda_k1[l], lambda_q2[l], lambda_k2[l]], axis=0)
    seg64 = _block_diag_ones(256, ATT_DH)
    seg128 = _block_diag_ones(256, GM_WIDTH // GM_GROUPS)
    bias_p = jnp.repeat(gm_bs[l].T, GM_WIDTH // GM_GROUPS, axis=1)
    ri = np.arange(GM_CHUNK)
    tril_p = jnp.asarray((ri[None, :] <= ri[:, None]).astype(np.float32))
    ws_s = jnp.tile(gm_ws[l][:, :t_s, :t_s], (1, GM_CHUNK // t_s, GM_CHUNK // t_s))
    mask_s = jnp.asarray(((ri[:, None] // t_s == ri[None, :] // t_s)
                          & (ri[None, :] % t_s <= ri[:, None] % t_s)).astype(np.float32))
    bias_s = jnp.tile(bias_p[:t_s], (GM_CHUNK // t_s, 1))

    n_c = bsz + nb
    c_all = jnp.concatenate([c_prompt, c_sample, jnp.zeros((-n_c % 8, d), F32)], axis=0)
    mod = _adaln(c_all, w_ada[l], b_ada[l])
    mod_p = mod[:bsz].reshape(bsz, 1, 6, d)
    mod_s = jnp.repeat(mod[bsz:n_c], t_s, axis=0).reshape(1, nb * t_s, 6, d)
    mp = [mod_p[:, :, i, :] for i in range(6)]
    ms_ = [mod_s[:, :, i, :] for i in range(6)]

    xp2 = x_prompt.reshape(bsz * t_p, d)
    cos_p, sin_p = _rope_tables(jnp.arange(t_p))
    q_p, k_p, kb_p, v_p, vb_p, u_p, vg_p, gates_p = _proj_in(
        xp2, mp[0], mp[1], g1n, w_in_b, cos_p, sin_p, gq, gk, gv, seg64, seg128,
        tm=512, rows_per_mod=t_p, vg_dtype=BF, k_transposed=True)
    a_p = _prompt_attention(lam_p, q_p.reshape(bsz, t_p, d), kb_p, vb_p.reshape(bsz, t_p, d), sg, lam_init)
    x1_p, h2_p, pq_p = _mix(xp2, a_p.reshape(bsz * t_p, d), u_p, vg_p, gates_p, mp[2], mp[3], mp[4], g2n,
                            gm_ws[l], tril_p, bias_p, wa_b, wb_b, wo_b, wq_b, tm=256, rows_per_mod=t_p)
    y_p = _peer(h2_p, pq_p, x1_p, mp[5], sk_b, u_b, v_b,
                tm_topk=128, tm_gates=128, tm_dense=1024, rows_per_mod=t_p)

    r_s = nb * t_s
    xs2 = x_sample.reshape(r_s, d)
    cos_s, sin_s = _rope_tables(past_len + jnp.arange(t_s))
    cos_st = jnp.tile(cos_s, (nb, 1))
    sin_st = jnp.tile(sin_s, (nb, 1))
    q_s, k_s, _, v_s, _, u_s, vg_s, gates_s = _proj_in(
        xs2, ms_[0], ms_[1], g1n, w_in_b, cos_st, sin_st, gq, gk, gv, seg64, seg128,
        tm=r_s, rows_per_mod=r_s, vg_dtype=F32, k_transposed=False)
    q3 = q_s.reshape(nb, t_s, d)
    q_rep = jnp.tile(q3, (1, DEC_COLS // t_s, 1))
    own_cols = jnp.asarray(np.arange(DEC_COLS)[:, None] // t_s == np.arange(d)[None, :] // ATT_DH)
    qbd = jnp.where(own_cols, q_rep, jnp.zeros_like(q_rep))
    pad8 = lambda a: jnp.pad(a.reshape(nb, t_s, d), ((0, 0), (0, 8 - t_s), (0, 0)))
    cache_kt = cache_k[l].transpose(0, 2, 3, 4, 1).reshape(n_pool, d, page)
    cache_vr = cache_v[l].reshape(n_pool, page * ATT_HEADS, ATT_VD)
    a_s = _sample_attention(page_table, lam_p, qbd, pad8(k_s), pad8(v_s), sg, cache_kt, cache_vr,
                            n_q=t_s, lam_init=lam_init)
    x1_s, h2_s, pq_s = _mix(xs2, a_s.reshape(r_s, d), u_s, vg_s, gates_s, ms_[2], ms_[3], ms_[4], g2n,
                            ws_s, mask_s, bias_s, wa_b, wb_b, wo_b, wq_b, tm=r_s, rows_per_mod=r_s)
    y_s = _peer(h2_s, pq_s, x1_s, ms_[5], sk_b, u_b, v_b,
                tm_topk=r_s, tm_gates=r_s, tm_dense=r_s, rows_per_mod=r_s)

    return (
        y_p.reshape(bsz, t_p, d),
        y_s.reshape(nb, t_s, d),
        k_p.reshape(1, bsz, ATT_HEADS, 2, ATT_DH, t_p).transpose(0, 1, 5, 2, 3, 4),
        v_p.reshape(1, bsz, t_p, ATT_HEADS, ATT_VD),
        k_s.reshape(1, nb, t_s, ATT_HEADS, 2, ATT_DH),
        v_s.reshape(1, nb, t_s, ATT_HEADS, ATT_VD),
        vg_s.reshape(1, nb, t_s, GM_WIDTH),
    )
```

```python
import functools
import math

import jax
import jax.numpy as jnp
import numpy as np
from jax import lax
from jax.experimental import pallas as pl
from jax.experimental.pallas import tpu as pltpu

BF = jnp.bfloat16
F32 = jnp.float32
I32 = jnp.int32

D_MODEL = 1024
ATT_HEADS = 8
ATT_DH = 64
ATT_VD = 128
ROPE_THETA = 10000.0
GM_GROUPS = 8
GM_CHUNK = 128
GM_WIDTH = 1024
IN_WIDTH = 7168
PEER_HEADS = 8
PEER_NKEYS = 128
PEER_TOPK = 16
PEER_SEL = PEER_HEADS * PEER_TOPK
EPS = 1e-6
NEG_INF = float("-inf")

V7X_VMEM_BYTES = 64 * 1024 * 1024
VMEM_COMPILER_RESERVE = 8 * 1024 * 1024
VMEM_LIMIT = V7X_VMEM_BYTES - VMEM_COMPILER_RESERVE

PROJ_ROWS = 512
ATTN_Q_ROWS = 512
MIX_ROWS = 256
TOPK_TOKENS = 128
GATE_TOKENS = 128
DENSE_TOKENS = 1024


def _cparams(sem):
    return pltpu.CompilerParams(dimension_semantics=sem, vmem_limit_bytes=VMEM_LIMIT)


def _gelu(x):
    return 0.5 * x * (1.0 + lax.erf(x * math.sqrt(0.5)))


def _seg_mean_sq(z, seg, n):
    z2 = z * z
    hi = z2.astype(BF)
    lo = (z2 - hi.astype(F32)).astype(BF)
    ss = jnp.dot(hi, seg, preferred_element_type=F32) + jnp.dot(lo, seg, preferred_element_type=F32)
    return ss * (1.0 / n)


def _ada_kernel(c_ref, w_ref, b_ref, o_ref):
    c = c_ref[...]
    s = c * jax.nn.sigmoid(c)
    o_ref[...] = jnp.dot(s.astype(BF), w_ref[...].astype(BF), preferred_element_type=F32) + b_ref[...]


def _adaln(c_all, w_ada, b_ada):
    r = c_all.shape[0]
    n = w_ada.shape[1]
    tn = 1024
    return pl.pallas_call(
        _ada_kernel,
        grid=(n // tn,),
        in_specs=[
            pl.BlockSpec((r, D_MODEL), lambda j: (0, 0)),
            pl.BlockSpec((D_MODEL, tn), lambda j: (0, j)),
            pl.BlockSpec((1, tn), lambda j: (0, j)),
        ],
        out_specs=pl.BlockSpec((r, tn), lambda j: (0, j)),
        out_shape=jax.ShapeDtypeStruct((r, n), F32),
        compiler_params=_cparams(("arbitrary",)),
        name="adaln",
    )(c_all, w_ada, b_ada.reshape(1, n))


def _proj_in_kernel(x_ref, sh_ref, sc_ref, g1_ref, w_ref, cos_ref, sin_ref, gq_ref, gk_ref, gv_ref,
                    seg64_ref, seg128_ref,
                    q_ref, k_ref, kb_ref, v_ref, vb_ref, u_ref, vg_ref, gate_ref, *, k_transposed):
    x = x_ref[...]
    ms = jnp.mean(x * x, axis=-1, keepdims=True)
    h = x * lax.rsqrt(ms + EPS) * g1_ref[...]
    h = h * (1.0 + sc_ref[0]) + sh_ref[0]
    hb = h.astype(BF)
    cos = cos_ref[...]
    sin = sin_ref[...]
    seg64 = seg64_ref[...]
    seg128 = seg128_ref[...]
    tm = x.shape[0]
    cw = 256
    lane = lax.broadcasted_iota(I32, (tm, cw), 1)
    first_half = (lane % ATT_DH) < (ATT_DH // 2)

    def qk_tile(col, g):
        z = jnp.dot(hb, w_ref[:, col:col + cw], preferred_element_type=F32)
        y = z * lax.rsqrt(_seg_mean_sq(z, seg64, ATT_DH) + EPS) * g
        rot = jnp.where(first_half, pltpu.roll(y, cw - ATT_DH // 2, 1), pltpu.roll(y, ATT_DH // 2, 1))
        return y * cos + rot * sin

    for t in range(D_MODEL // cw):
        c = t * cw
        q = qk_tile(c, gq_ref[...])
        q_ref[:, c:c + cw] = (q * (ATT_DH ** -0.5)).astype(BF)
        k = qk_tile(D_MODEL + c, gk_ref[...])
        if k_transposed:
            kt = k.T
            k_ref[c:c + cw, :] = kt
            kb_ref[c:c + cw, :] = kt.astype(BF)
        else:
            k_ref[:, c:c + cw] = k
            kb_ref[:, c:c + cw] = k.astype(BF)
        v = jnp.dot(hb, w_ref[:, 2 * D_MODEL + c:2 * D_MODEL + c + cw], preferred_element_type=F32)
        v_ref[:, c:c + cw] = v
        vb_ref[:, c:c + cw] = v.astype(BF)
        zu = jnp.dot(hb, w_ref[:, 3 * D_MODEL + c:3 * D_MODEL + c + cw], preferred_element_type=F32)
        u_ref[:, c:c + cw] = _gelu(zu).astype(BF)
        zv = jnp.dot(hb, w_ref[:, 4 * D_MODEL + c:4 * D_MODEL + c + cw], preferred_element_type=F32)
        gv = _gelu(zv)
        vg = gv * lax.rsqrt(_seg_mean_sq(gv, seg128, GM_WIDTH // GM_GROUPS) + EPS) * gv_ref[...]
        vg_ref[:, c:c + cw] = vg.astype(vg_ref.dtype)
    for t in range(2 * D_MODEL // cw):
        c = t * cw
        zg = jnp.dot(hb, w_ref[:, 5 * D_MODEL + c:5 * D_MODEL + c + cw], preferred_element_type=F32)
        gate_ref[:, c:c + cw] = jax.nn.sigmoid(zg).astype(BF)


def _proj_in(x2, sh, sc, g1, w_in_b, cos, sin, gq, gk, gv, seg64, seg128, tm, rows_per_mod, vg_dtype,
             k_transposed):
    r = x2.shape[0]
    rg = sh.shape[1]
    mod_spec = pl.BlockSpec((1, rg, D_MODEL), lambda i: ((i * tm) // rows_per_mod, 0, 0))
    row = lambda w: pl.BlockSpec((tm, w), lambda i: (i, 0))
    full = lambda a: pl.BlockSpec(a.shape, lambda i: (0,) * a.ndim, pipeline_mode=pl.Buffered(1))
    rope_tiles = cos.shape[0] // tm
    rope_spec = pl.BlockSpec((tm, cos.shape[1]), lambda i: (i % rope_tiles, 0))
    if k_transposed:
        tiles = rows_per_mod // tm
        k_shape = (r // rows_per_mod, D_MODEL, rows_per_mod)
        k_spec = pl.BlockSpec((None, D_MODEL, tm), lambda i: (i // tiles, 0, i % tiles))
    else:
        k_shape = (r, D_MODEL)
        k_spec = row(D_MODEL)
    out_shapes = [
        jax.ShapeDtypeStruct((r, D_MODEL), BF),
        jax.ShapeDtypeStruct(k_shape, F32),
        jax.ShapeDtypeStruct(k_shape, BF),
        jax.ShapeDtypeStruct((r, D_MODEL), F32),
        jax.ShapeDtypeStruct((r, D_MODEL), BF),
        jax.ShapeDtypeStruct((r, D_MODEL), BF),
        jax.ShapeDtypeStruct((r, D_MODEL), vg_dtype),
        jax.ShapeDtypeStruct((r, 2 * D_MODEL), BF),
    ]
    return pl.pallas_call(
        functools.partial(_proj_in_kernel, k_transposed=k_transposed),
        grid=(r // tm,),
        in_specs=[row(D_MODEL), mod_spec, mod_spec, full(g1), full(w_in_b), rope_spec, rope_spec,
                  full(gq), full(gk), full(gv), full(seg64), full(seg128)],
        out_specs=[row(D_MODEL), k_spec, k_spec] + [row(D_MODEL)] * 4 + [row(2 * D_MODEL)],
        out_shape=out_shapes,
        compiler_params=_cparams(("arbitrary",)),
        name="proj_in",
    )(x2, sh, sc, g1, w_in_b, cos, sin, gq, gk, gv, seg64, seg128)


def _lambda_value(lam_ref, lam_init):
    lp = lam_ref[...]
    a = jnp.sum(lp[0:1] * lp[1:2], axis=-1, keepdims=True)
    b = jnp.sum(lp[2:3] * lp[3:4], axis=-1, keepdims=True)
    return jnp.exp(a) - jnp.exp(b) + lam_init


def _subln(o, sg, lam_init):
    ss = jnp.mean(o * o, axis=-1, keepdims=True)
    return o * lax.rsqrt(ss + EPS) * sg * (1.0 - lam_init)


def _attn_kernel(lam_ref, q_ref, k_ref, v_ref, sg_ref, o_ref, vx_sc, m_sc, acc_sc, *, tq, lam_init):
    qi = pl.program_id(2)
    t = v_ref.shape[0]

    @pl.when(qi == 0)
    def _():
        vx_sc[:, :ATT_VD] = v_ref[...]
        vx_sc[:, ATT_VD:] = jnp.ones((t, ATT_VD), BF)

    q = q_ref[...]
    lane = lax.broadcasted_iota(I32, q.shape, 1)
    zero = jnp.zeros_like(q)
    q2 = jnp.concatenate([jnp.where(lane < ATT_DH, q, zero), jnp.where(lane >= ATT_DH, q, zero)], axis=0)
    m_sc[...] = jnp.full(m_sc.shape, NEG_INF, F32)
    acc_sc[...] = jnp.zeros(acc_sc.shape, F32)

    def step(j, masked):
        start = pl.multiple_of(j * tq, tq)
        s = jnp.dot(q2, k_ref[:, pl.ds(start, tq)], preferred_element_type=F32)
        if masked:
            r = lax.broadcasted_iota(I32, s.shape, 0)
            c = lax.broadcasted_iota(I32, s.shape, 1)
            qpos = jnp.where(r >= tq, r - tq, r)
            s = jnp.where(c <= qpos, s, NEG_INF)
        m_prev = m_sc[...]
        m_new = jnp.maximum(m_prev, jnp.max(s, axis=-1, keepdims=True))
        alpha = jnp.exp(m_prev - m_new)
        p = jnp.exp(s - jnp.tile(m_new, (1, tq // ATT_VD)))
        pv = jnp.dot(p.astype(BF), vx_sc[pl.ds(start, tq), :], preferred_element_type=F32)
        acc_sc[...] = jnp.tile(alpha, (1, 2)) * acc_sc[...] + pv
        m_sc[...] = m_new

    def body(j, carry):
        step(j, False)
        return carry

    lax.fori_loop(0, qi, body, 0)
    step(qi, True)

    lam = _lambda_value(lam_ref, lam_init)
    o_all = acc_sc[:, :ATT_VD] / acc_sc[:, ATT_VD:]
    o = o_all[:tq] - lam * o_all[tq:]
    o_ref[...] = _subln(o, sg_ref[...], lam_init).astype(o_ref.dtype)


def _prompt_attention(lam_p, qb, kbt, vb, sg, lam_init, tq=ATTN_Q_ROWS):
    b, t, _ = qb.shape
    kern = functools.partial(_attn_kernel, tq=tq, lam_init=lam_init)
    return pl.pallas_call(
        kern,
        grid=(b, ATT_HEADS, t // tq),
        in_specs=[
            pl.BlockSpec(lam_p.shape, lambda bi, h, i: (0, 0)),
            pl.BlockSpec((None, tq, ATT_VD), lambda bi, h, i: (bi, i, h)),
            pl.BlockSpec((None, ATT_VD, t), lambda bi, h, i: (bi, h, 0)),
            pl.BlockSpec((None, t, ATT_VD), lambda bi, h, i: (bi, 0, h)),
            pl.BlockSpec(sg.shape, lambda bi, h, i: (0, 0)),
        ],
        out_specs=pl.BlockSpec((None, tq, ATT_VD), lambda bi, h, i: (bi, i, h)),
        out_shape=jax.ShapeDtypeStruct((b, t, ATT_HEADS * ATT_VD), BF),
        scratch_shapes=[pltpu.VMEM((t, 2 * ATT_VD), BF), pltpu.VMEM((2 * tq, ATT_VD), F32),
                        pltpu.VMEM((2 * tq, 2 * ATT_VD), F32)],
        compiler_params=_cparams(("arbitrary", "arbitrary", "arbitrary")),
        name="prompt_attn",
    )(lam_p, qb, kbt, vb, sg)


DEC_PP = 8
DEC_COLS = 64


def _dec_attn_kernel(pt_ref, lam_ref, qbd_ref, kn_ref, vn_ref, sg_ref, exp_ref, hm_ref, *rest, n_q, lam_init):
    k_refs = rest[:DEC_PP]
    v_refs = rest[DEC_PP:2 * DEC_PP]
    o_ref = rest[2 * DEC_PP]
    kcat_sc, m_sc, l_sc, acc_sc = rest[2 * DEC_PP + 1:]
    g = pl.program_id(1)
    qbd = qbd_ref[0]
    rows = lax.broadcasted_iota(I32, (DEC_COLS, ATT_VD), 0)
    page = k_refs[0].shape[2]

    @pl.when(g == 0)
    def _():
        m_sc[...] = jnp.full(m_sc.shape, NEG_INF, F32)
        l_sc[...] = jnp.zeros(l_sc.shape, F32)
        acc_sc[...] = jnp.zeros(acc_sc.shape, F32)

    def rescale(s_list):
        m_prev = m_sc[...]
        m_new = m_prev
        for s in s_list:
            m_new = jnp.maximum(m_new, jnp.max(s, axis=-1, keepdims=True))
        alpha = jnp.exp(m_prev - m_new)
        m_sc[...] = m_new
        return m_new, alpha * l_sc[...], alpha * acc_sc[...]

    for j in range(DEC_PP):
        kcat_sc[:, j * page:(j + 1) * page] = k_refs[j][0].astype(BF)
    s = jnp.dot(qbd, kcat_sc[...], preferred_element_type=F32)
    m_new, l_new, acc = rescale([s])
    p = jnp.exp(s - m_new)
    l_sc[...] = l_new + jnp.sum(p, axis=-1, keepdims=True)
    p_rows = jnp.concatenate([p[:, j * page:(j + 1) * page] for j in range(DEC_PP)], axis=0).astype(BF)
    pe = jnp.dot(p_rows, exp_ref[...], preferred_element_type=F32).astype(BF) * hm_ref[...]
    for j in range(DEC_PP):
        acc = acc + jnp.dot(pe[j * DEC_COLS:(j + 1) * DEC_COLS], v_refs[j][0].astype(BF),
                            preferred_element_type=F32)
    acc_sc[...] = acc

    @pl.when(g == pl.num_programs(1) - 1)
    def _():
        kn = kn_ref[0].astype(BF)
        s = lax.dot_general(qbd, kn, (((1,), (1,)), ((), ())), preferred_element_type=F32)
        r = lax.broadcasted_iota(I32, s.shape, 0)
        c = lax.broadcasted_iota(I32, s.shape, 1)
        s = jnp.where(c <= (r % n_q), s, NEG_INF)
        m_fin, l_fin, acc_fin = rescale([s])
        p = jnp.exp(s - m_fin)
        l_fin = l_fin + jnp.sum(p, axis=-1, keepdims=True)
        pv = jnp.dot(p.astype(BF), vn_ref[0].astype(BF), preferred_element_type=F32)
        for h in range(ATT_HEADS):
            own = (rows // (2 * n_q)) == h
            acc_fin = acc_fin + jnp.where(own, pv[:, h * ATT_VD:(h + 1) * ATT_VD], 0.0)
        lam = _lambda_value(lam_ref, lam_init)
        o_all = acc_fin / l_fin
        for h in range(ATT_HEADS):
            blk = o_all[h * 2 * n_q:(h + 1) * 2 * n_q]
            o = blk[:n_q] - lam * blk[n_q:]
            o_ref[0, :, h * ATT_VD:(h + 1) * ATT_VD] = _subln(o, sg_ref[...], lam_init).astype(o_ref.dtype)


def _sample_attention(page_table, lam_p, qbd, kn, vn, sg, cache_kt, cache_vr, n_q, lam_init):
    nb, n_pages = page_table.shape
    width = cache_kt.shape[1]
    page = cache_kt.shape[2]
    col = np.arange(page * ATT_HEADS)
    expand = jnp.asarray((col[None, :] // ATT_HEADS == np.arange(page)[:, None]).astype(BF))
    row_head = (np.arange(DEC_PP * DEC_COLS) % DEC_COLS) // (2 * n_q)
    hmask = jnp.asarray((col[None, :] % ATT_HEADS == row_head[:, None]).astype(BF))

    def page_spec(arr, j):
        return pl.BlockSpec((1,) + arr.shape[1:], lambda b, g, pt: (pt[b, g * DEC_PP + j], 0, 0))

    const = lambda a: pl.BlockSpec(a.shape, lambda b, g, pt: (0,) * a.ndim)
    kern = functools.partial(_dec_attn_kernel, n_q=n_q, lam_init=lam_init)
    grid_spec = pltpu.PrefetchScalarGridSpec(
        num_scalar_prefetch=1,
        grid=(nb, n_pages // DEC_PP),
        in_specs=[
            const(lam_p),
            pl.BlockSpec((1, DEC_COLS, width), lambda b, g, pt: (b, 0, 0)),
            pl.BlockSpec((1, 8, width), lambda b, g, pt: (b, 0, 0)),
            pl.BlockSpec((1, 8, width), lambda b, g, pt: (b, 0, 0)),
            const(sg), const(expand), const(hmask),
        ] + [page_spec(cache_kt, j) for j in range(DEC_PP)] + [page_spec(cache_vr, j) for j in range(DEC_PP)],
        out_specs=pl.BlockSpec((1, n_q, width), lambda b, g, pt: (b, 0, 0)),
        scratch_shapes=[pltpu.VMEM((width, DEC_PP * page), BF),
                        pltpu.VMEM((DEC_COLS, 1), F32), pltpu.VMEM((DEC_COLS, 1), F32),
                        pltpu.VMEM((DEC_COLS, ATT_VD), F32)],
    )
    return pl.pallas_call(
        kern,
        grid_spec=grid_spec,
        out_shape=jax.ShapeDtypeStruct((nb, n_q, width), BF),
        compiler_params=_cparams(("arbitrary", "arbitrary")),
        name="sample_attn",
    )(page_table, lam_p, qbd, kn, vn, sg, expand, hmask, *([cache_kt] * DEC_PP), *([cache_vr] * DEC_PP))


def _mix_kernel(x_ref, a_ref, u_ref, vg_ref, gate_ref, g1_ref, sh2_ref, sc2_ref, n2_ref,
                ws_ref, wmask_ref, bias_ref, wa_ref, wb_ref, wo_ref, wq_ref, x1_ref, h2_ref, pq_ref):
    tm = x_ref.shape[0]
    gc = GM_WIDTH // GM_GROUPS
    wmask = wmask_ref[...]
    bias = bias_ref[...]
    s_chunks = []
    for c in range(tm // GM_CHUNK):
        rows = slice(c * GM_CHUNK, (c + 1) * GM_CHUNK)
        cols = []
        for g in range(GM_GROUPS):
            w = (ws_ref[g] * wmask).astype(BF)
            vv = vg_ref[rows, g * gc:(g + 1) * gc].astype(BF)
            cols.append(jnp.dot(w, vv, preferred_element_type=F32))
        mixed = jnp.concatenate(cols, axis=1) + bias
        s_chunks.append((u_ref[rows, :].astype(F32) * mixed).astype(BF))
    s_out = jnp.concatenate(s_chunks, axis=0) if len(s_chunks) > 1 else s_chunks[0]
    ya = jnp.dot(a_ref[...], wa_ref[...], preferred_element_type=F32)
    yb = jnp.dot(s_out, wb_ref[...], preferred_element_type=F32)
    ga = gate_ref[:, :D_MODEL].astype(F32)
    gb = gate_ref[:, D_MODEL:].astype(F32)
    mix = (ga * ya + gb * yb).astype(BF)
    y = jnp.dot(mix, wo_ref[...], preferred_element_type=F32)
    x1 = x_ref[...] + g1_ref[0] * y
    x1_ref[...] = x1
    ms = jnp.mean(x1 * x1, axis=-1, keepdims=True)
    h2 = x1 * lax.rsqrt(ms + EPS) * n2_ref[...]
    h2 = h2 * (1.0 + sc2_ref[0]) + sh2_ref[0]
    h2b = h2.astype(BF)
    h2_ref[...] = h2b
    pq_ref[...] = jnp.dot(h2b, wq_ref[...], preferred_element_type=F32).astype(BF)


def _mix(x2, a_out, u, vg, gates, g1, sh2, sc2, n2, ws_exp, wmask, bias_full, wa, wb, wo, wq, tm, rows_per_mod):
    r = x2.shape[0]
    rg = g1.shape[1]
    mod_spec = pl.BlockSpec((1, rg, D_MODEL), lambda i: ((i * tm) // rows_per_mod, 0, 0))
    row = lambda w: pl.BlockSpec((tm, w), lambda i: (i, 0))
    full = lambda a: pl.BlockSpec(a.shape, lambda i: (0,) * a.ndim)
    return pl.pallas_call(
        _mix_kernel,
        grid=(r // tm,),
        in_specs=[row(D_MODEL), row(D_MODEL), row(D_MODEL), row(D_MODEL), row(2 * D_MODEL),
                  mod_spec, mod_spec, mod_spec, full(n2), full(ws_exp), full(wmask), full(bias_full),
                  full(wa), full(wb), full(wo), full(wq)],
        out_specs=[row(D_MODEL), row(D_MODEL), row(wq.shape[1])],
        out_shape=[jax.ShapeDtypeStruct((r, D_MODEL), F32), jax.ShapeDtypeStruct((r, D_MODEL), BF),
                   jax.ShapeDtypeStruct((r, wq.shape[1]), BF)],
        compiler_params=_cparams(("arbitrary",)),
        name="mix",
    )(x2, a_out, u, vg, gates, g1, sh2, sc2, n2, ws_exp, wmask, bias_full, wa, wb, wo, wq)


def _top_rows(s, iota, n_rows, count, val_ref, idx_ref):
    for r in range(count):
        m = jnp.max(s, axis=0, keepdims=True)
        idx = jnp.min(jnp.where(s == m, iota, n_rows), axis=0, keepdims=True)
        val_ref[r:r + 1, :] = m
        idx_ref[r:r + 1, :] = idx
        s = jnp.where(iota == idx, NEG_INF, s)


def _pair_candidates(v1, j1, v2, j2, sub8):
    nk = PEER_NKEYS
    v2_16, j2_16 = v2[...], j2[...]
    v2_8, j2_8 = v2[0:8, :], j2[0:8, :]
    cand = [v1[0:1, :] + v2_16]
    cidx = [j1[0:1, :] * nk + j2_16]
    for a in range(1, 8):
        ok = sub8 < (PEER_TOPK // (a + 1))
        cand.append(jnp.where(ok, v1[a:a + 1, :] + v2_8, NEG_INF))
        cidx.append(j1[a:a + 1, :] * nk + j2_8)
    cand.append(v1[8:16, :] + v2[0:1, :])
    cidx.append(j1[8:16, :] * nk + j2[0:1, :])
    return jnp.concatenate(cand, axis=0), jnp.concatenate(cidx, axis=0)


PEER_NCAND = 80
def _peer_topk_kernel(q_ref, sk_ref, i1_ref, i2_ref, g_ref, val_sc, idx_sc, ts_sc, eid_sc):
    tm = q_ref.shape[0]
    nk = PEER_NKEYS
    iota_k = lax.broadcasted_iota(I32, (nk, tm), 0).astype(F32)
    iota_c = lax.broadcasted_iota(I32, (PEER_NCAND, tm), 0).astype(F32)
    sub8 = lax.broadcasted_iota(I32, (8, tm), 0)

    for hp in range(2 * PEER_HEADS):
        s = lax.dot_general(sk_ref[hp], q_ref[:, hp * nk:(hp + 1) * nk], (((1,), (1,)), ((), ())),
                            preferred_element_type=F32)
        _top_rows(s, iota_k, nk, PEER_TOPK, val_sc.at[hp], idx_sc.at[hp])
    for h in range(PEER_HEADS):
        cand, cidx = _pair_candidates(val_sc.at[2 * h], idx_sc.at[2 * h],
                                      val_sc.at[2 * h + 1], idx_sc.at[2 * h + 1], sub8)
        for r in range(PEER_TOPK):
            m = jnp.max(cand, axis=0, keepdims=True)
            pos = jnp.min(jnp.where(cand == m, iota_c, float(PEER_NCAND)), axis=0, keepdims=True)
            hit = iota_c == pos
            eid_sc[h, r:r + 1, :] = jnp.max(jnp.where(hit, cidx, -1.0), axis=0, keepdims=True)
            ts_sc[h, r:r + 1, :] = m
            cand = jnp.where(hit, NEG_INF, cand)
        ts = ts_sc[h]
        e = jnp.exp(ts - ts[0:1, :])
        gates = e / jnp.sum(e, axis=0, keepdims=True)
        eid = eid_sc[h].astype(I32)
        rows = slice(h * PEER_TOPK, (h + 1) * PEER_TOPK)
        i1_ref[rows, :] = lax.shift_right_logical(eid, 7).astype(F32)
        i2_ref[rows, :] = lax.bitwise_and(eid, nk - 1).astype(F32)
        g_ref[rows, :] = gates


def _peer_topk(q, sk, tm):
    t, nq = q.shape
    out = jax.ShapeDtypeStruct((PEER_SEL, t), F32)
    col = pl.BlockSpec((PEER_SEL, tm), lambda i: (0, i))
    return pl.pallas_call(
        _peer_topk_kernel,
        grid=(t // tm,),
        in_specs=[pl.BlockSpec((tm, nq), lambda i: (i, 0)),
                  pl.BlockSpec(sk.shape, lambda i: (0, 0, 0))],
        out_specs=[col, col, col],
        out_shape=[out, out, out],
        scratch_shapes=[pltpu.VMEM((2 * PEER_HEADS, PEER_TOPK, tm), F32),
                        pltpu.VMEM((2 * PEER_HEADS, PEER_TOPK, tm), F32),
                        pltpu.VMEM((PEER_HEADS, PEER_TOPK, tm), F32),
                        pltpu.VMEM((PEER_HEADS, PEER_TOPK, tm), F32)],
        compiler_params=_cparams(("arbitrary",)),
        name="peer_topk",
    )(q, sk)


GATE_TG = 32


def _peer_gates_kernel(i1_ref, i2_ref, g_ref, perm_ref, o_ref, scr):
    tm = i1_ref.shape[0]
    nk = PEER_NKEYS
    ng = nk // 8
    iota = lax.broadcasted_iota(I32, (nk, PEER_SEL), 0).astype(F32)

    def body(t, carry):
        r1 = i1_ref[pl.ds(t, 1), :]
        r2 = i2_ref[pl.ds(t, 1), :]
        gg = g_ref[pl.ds(t, 1), :]
        a = jnp.where(iota == r1, gg, 0.0).astype(BF)
        b = jnp.where(iota == r2, 1.0, 0.0).astype(BF)
        gt = lax.dot_general(a, b, (((1,), (1,)), ((), ())), preferred_element_type=F32)
        row0 = pl.multiple_of(t * 8, 8)
        for q in range(ng):
            scr[q, pl.ds(row0, 8), :] = gt[q * 8:(q + 1) * 8, :]
        return carry

    lax.fori_loop(0, tm, body, 0, unroll=True)
    perm = perm_ref[...]
    rows = GATE_TG * 8
    for q in range(0, ng, 2):
        for c in range(tm // GATE_TG):
            x = jnp.concatenate([scr[q, c * rows:(c + 1) * rows, :], scr[q + 1, c * rows:(c + 1) * rows, :]],
                                axis=1).astype(BF)
            y = jnp.dot(perm, x, preferred_element_type=F32).astype(o_ref.dtype)
            for half in range(2):
                for il in range(8):
                    o_ref[(q + half) * 8 + il, c * GATE_TG:(c + 1) * GATE_TG, :] = (
                        y[il * GATE_TG:(il + 1) * GATE_TG, half * nk:(half + 1) * nk])


def _peer_gates(i1, i2, g, perm, tm):
    t = i1.shape[0]
    row = pl.BlockSpec((tm, PEER_SEL), lambda i: (i, 0))
    return pl.pallas_call(
        _peer_gates_kernel,
        grid=(t // tm,),
        in_specs=[row, row, row, pl.BlockSpec(perm.shape, lambda i: (0, 0))],
        out_specs=pl.BlockSpec((PEER_NKEYS, tm, PEER_NKEYS), lambda i: (0, i, 0)),
        out_shape=jax.ShapeDtypeStruct((PEER_NKEYS, t, PEER_NKEYS), BF),
        scratch_shapes=[pltpu.VMEM((PEER_NKEYS // 8, tm * 8, PEER_NKEYS), F32)],
        compiler_params=_cparams(("arbitrary",)),
        name="peer_gates",
    )(i1, i2, g, perm)


PEER_EB = 16


def _peer_dense_kernel(h2_ref, x1_ref, g2_ref, gm_ref, ut_ref, v_ref, o_ref, acc_sc):
    e = pl.program_id(1)

    @pl.when(e == 0)
    def _():
        acc_sc[...] = jnp.zeros(acc_sc.shape, F32)

    h2 = h2_ref[...]
    cw = 2 * PEER_NKEYS
    ws = []
    for c in range(PEER_EB // 2):
        a = jnp.dot(h2, ut_ref[:, c * cw:(c + 1) * cw], preferred_element_type=F32)
        gate = jnp.concatenate([gm_ref[2 * c], gm_ref[2 * c + 1]], axis=1).astype(F32)
        ws.append((gate * _gelu(a)).astype(BF))
    w = jnp.concatenate(ws, axis=1)
    acc_sc[...] += jnp.dot(w, v_ref[...], preferred_element_type=F32)

    @pl.when(e == pl.num_programs(1) - 1)
    def _():
        o_ref[...] = x1_ref[...] + g2_ref[0] * acc_sc[...]


def _peer_dense(h2, x1, g2, gmat, ut_b, v_b, tm, rows_per_mod):
    t = h2.shape[0]
    ne = v_b.shape[0]
    eb = PEER_EB * PEER_NKEYS
    rg = g2.shape[1]
    return pl.pallas_call(
        _peer_dense_kernel,
        grid=(t // tm, ne // eb),
        in_specs=[pl.BlockSpec((tm, D_MODEL), lambda i, e: (i, 0)),
                  pl.BlockSpec((tm, D_MODEL), lambda i, e: (i, 0)),
                  pl.BlockSpec((1, rg, D_MODEL), lambda i, e: ((i * tm) // rows_per_mod, 0, 0)),
                  pl.BlockSpec((PEER_EB, tm, PEER_NKEYS), lambda i, e: (e, i, 0)),
                  pl.BlockSpec((D_MODEL, eb), lambda i, e: (0, e)),
                  pl.BlockSpec((eb, D_MODEL), lambda i, e: (e, 0))],
        out_specs=pl.BlockSpec((tm, D_MODEL), lambda i, e: (i, 0)),
        out_shape=jax.ShapeDtypeStruct((t, D_MODEL), F32),
        scratch_shapes=[pltpu.VMEM((tm, D_MODEL), F32)],
        compiler_params=_cparams(("arbitrary", "arbitrary")),
        name="peer_dense",
    )(h2, x1, g2, gmat, ut_b, v_b)


def _rope_tables(pos, width=256):
    inv = 1.0 / (ROPE_THETA ** (jnp.arange(0, ATT_DH, 2, dtype=F32) / ATT_DH))
    ang = pos.astype(F32)[:, None] * inv[None, :]
    cos = jnp.cos(ang)
    sin = jnp.sin(ang)
    reps = width // ATT_DH
    cos_t = jnp.tile(jnp.concatenate([cos, cos], axis=1), (1, reps))
    sin_t = jnp.tile(jnp.concatenate([-sin, sin], axis=1), (1, reps))
    return cos_t, sin_t


def _block_diag_ones(width, seg):
    r = np.arange(width) // seg
    return jnp.asarray((r[:, None] == r[None, :]).astype(BF))


def _peer(h2, pq, x1, g2, sk, u_b, v_b, tm_topk, tm_gates, tm_dense, rows_per_mod):
    i1t, i2t, gt = _peer_topk(pq, sk, tm_topk)
    r_out = np.arange(GATE_TG * 8)
    src = (r_out % GATE_TG) * 8 + r_out // GATE_TG
    perm = jnp.asarray((np.arange(GATE_TG * 8)[None, :] == src[:, None]).astype(BF))
    gmat = _peer_gates(i1t.T, i2t.T, gt.T, perm, tm_gates)
    return _peer_dense(h2, x1, g2, gmat, u_b, v_b, tm_dense, rows_per_mod)


def kernel(x_prompt, x_sample, cache_k, cache_v, page_table, c_prompt, c_sample, w_ada, b_ada, norm1_g, w_in, q_norm_g, k_norm_g, lambda_q1, lambda_k1, lambda_q2, lambda_k2, subln_g, gm_v_norm_g, gm_ws, gm_bs, w_att_out, w_gm_out, w_out, norm2_g, peer_w_q, peer_subkeys, peer_u, peer_v):
    depth = w_ada.shape[0]
    assert depth == 1
    l = 0
    lam_init = 0.8 - 0.6 * math.exp(-0.3 * l)
    bsz, t_p, d = x_prompt.shape
    nb, t_s, _ = x_sample.shape
    n_pool, page = cache_k.shape[1], cache_k.shape[2]
    past_len = page_table.shape[1] * page

    w_in_b = w_in[l].astype(BF)
    wa_b = w_att_out[l].astype(BF)
    wb_b = w_gm_out[l].astype(BF)
    wo_b = w_out[l].astype(BF)
    wq_b = peer_w_q[l].astype(BF)
    sk_b = peer_subkeys[l].reshape(2 * PEER_HEADS, PEER_NKEYS, -1).astype(BF)
    u_b = peer_u[l].astype(BF).T
    v_b = peer_v[l].astype(BF)
    g1n = norm1_g[l].reshape(1, d)
    g2n = norm2_g[l].reshape(1, d)
    gq = jnp.tile(q_norm_g[l], 256 // ATT_DH).reshape(1, 256)
    gk = jnp.tile(k_norm_g[l], 256 // ATT_DH).reshape(1, 256)
    gv = jnp.tile(gm_v_norm_g[l], 2).reshape(1, 256)
    sg = subln_g[l].reshape(1, ATT_VD)
    lam_p = jnp.stack([lambda_q1[l], lambda_k1[l], lambda_q2[l], lambda_k2[l]], axis=0)
    seg64 = _block_diag_ones(256, ATT_DH)
    seg128 = _block_diag_ones(256, GM_WIDTH // GM_GROUPS)
    bias_p = jnp.repeat(gm_bs[l].T, GM_WIDTH // GM_GROUPS, axis=1)
    ri = np.arange(GM_CHUNK)
    tril_p = jnp.asarray((ri[None, :] <= ri[:, None]).astype(np.float32))
    ws_s = jnp.tile(gm_ws[l][:, :t_s, :t_s], (1, GM_CHUNK // t_s, GM_CHUNK // t_s))
    mask_s = jnp.asarray(((ri[:, None] // t_s == ri[None, :] // t_s)
                          & (ri[None, :] % t_s <= ri[:, None] % t_s)).astype(np.float32))
    bias_s = jnp.tile(bias_p[:t_s], (GM_CHUNK // t_s, 1))

    n_c = bsz + nb
    c_all = jnp.concatenate([c_prompt, c_sample, jnp.zeros((-n_c % 8, d), F32)], axis=0)
    mod = _adaln(c_all, w_ada[l], b_ada[l])
    mod_p = mod[:bsz].reshape(bsz, 1, 6, d)
    mod_s = jnp.repeat(mod[bsz:n_c], t_s, axis=0).reshape(1, nb * t_s, 6, d)
    mp = [mod_p[:, :, i, :] for i in range(6)]
    ms_ = [mod_s[:, :, i, :] for i in range(6)]

    xp2 = x_prompt.reshape(bsz * t_p, d)
    cos_p, sin_p = _rope_tables(jnp.arange(t_p))
    q_p, k_p, kb_p, v_p, vb_p, u_p, vg_p, gates_p = _proj_in(
        xp2, mp[0], mp[1], g1n, w_in_b, cos_p, sin_p, gq, gk, gv, seg64, seg128,
        tm=PROJ_ROWS, rows_per_mod=t_p, vg_dtype=BF, k_transposed=True)
    a_p = _prompt_attention(lam_p, q_p.reshape(bsz, t_p, d), kb_p, vb_p.reshape(bsz, t_p, d), sg, lam_init)
    x1_p, h2_p, pq_p = _mix(xp2, a_p.reshape(bsz * t_p, d), u_p, vg_p, gates_p, mp[2], mp[3], mp[4], g2n,
                            gm_ws[l], tril_p, bias_p, wa_b, wb_b, wo_b, wq_b, tm=MIX_ROWS, rows_per_mod=t_p)
    y_p = _peer(h2_p, pq_p, x1_p, mp[5], sk_b, u_b, v_b,
                tm_topk=TOPK_TOKENS, tm_gates=GATE_TOKENS, tm_dense=DENSE_TOKENS, rows_per_mod=t_p)

    r_s = nb * t_s
    xs2 = x_sample.reshape(r_s, d)
    cos_s, sin_s = _rope_tables(past_len + jnp.arange(t_s))
    cos_st = jnp.tile(cos_s, (nb, 1))
    sin_st = jnp.tile(sin_s, (nb, 1))
    q_s, k_s, _, v_s, _, u_s, vg_s, gates_s = _proj_in(
        xs2, ms_[0], ms_[1], g1n, w_in_b, cos_st, sin_st, gq, gk, gv, seg64, seg128,
        tm=r_s, rows_per_mod=r_s, vg_dtype=F32, k_transposed=False)
    q3 = q_s.reshape(nb, t_s, d)
    q_rep = jnp.tile(q3, (1, DEC_COLS // t_s, 1))
    own_cols = jnp.asarray(np.arange(DEC_COLS)[:, None] // t_s == np.arange(d)[None, :] // ATT_DH)
    qbd = jnp.where(own_cols, q_rep, jnp.zeros_like(q_rep))
    pad8 = lambda a: jnp.pad(a.reshape(nb, t_s, d), ((0, 0), (0, 8 - t_s), (0, 0)))
    cache_kt = cache_k[l].transpose(0, 2, 3, 4, 1).reshape(n_pool, d, page)
    cache_vr = cache_v[l].reshape(n_pool, page * ATT_HEADS, ATT_VD)
    a_s = _sample_attention(page_table, lam_p, qbd, pad8(k_s), pad8(v_s), sg, cache_kt, cache_vr,
                            n_q=t_s, lam_init=lam_init)
    x1_s, h2_s, pq_s = _mix(xs2, a_s.reshape(r_s, d), u_s, vg_s, gates_s, ms_[2], ms_[3], ms_[4], g2n,
                            ws_s, mask_s, bias_s, wa_b, wb_b, wo_b, wq_b, tm=r_s, rows_per_mod=r_s)
    y_s = _peer(h2_s, pq_s, x1_s, ms_[5], sk_b, u_b, v_b,
                tm_topk=r_s, tm_gates=r_s, tm_dense=r_s, rows_per_mod=r_s)

    return (
        y_p.reshape(bsz, t_p, d),
        y_s.reshape(nb, t_s, d),
        k_p.reshape(1, bsz, ATT_HEADS, 2, ATT_DH, t_p).transpose(0, 1, 5, 2, 3, 4),
        v_p.reshape(1, bsz, t_p, ATT_HEADS, ATT_VD),
        k_s.reshape(1, nb, t_s, ATT_HEADS, 2, ATT_DH),
        v_s.reshape(1, nb, t_s, ATT_HEADS, ATT_VD),
        vg_s.reshape(1, nb, t_s, GM_WIDTH),
    )
```
